```python
import jax
import jax.numpy as jnp
from jax import lax
import numpy as np

D_MODEL = 1024
BATCH = 2
SEQ = 8192
DEPTH = 2
DEC_BATCH = 128
DEC_SEQ = 1
PAST_LEN = 2048
PAGE_SIZE = 128

HEAD_DIM = 64
MIX_W = D_MODEL
X_HEADS = 4
X_W = X_HEADS * HEAD_DIM
TOK_W = MIX_W - X_W
SB_HEADS = TOK_W // HEAD_DIM
SB_BLOCK = 128
SB_BIAS_INIT = -6.0
GLA_HEADS = 4
GLA_DV = TOK_W
GLA_DK = GLA_DV // 2
GLA_HDK = GLA_DK // GLA_HEADS
GLA_HDV = GLA_DV // GLA_HEADS
GLA_GATE_RANK = 16
GLA_TAU = 16.0
GLA_CHUNK = 64
N_MEM = 256
D_FF = 11 * D_MODEL // 4
N_EXPERTS = 8
TOP_K = 2
D_FF_EXPERT = 7 * D_MODEL // 2
RMS_EPS = 1e-6
N_SB = (DEPTH + 1) // 2
N_GLA = DEPTH // 2
N_DENSE = (DEPTH + 1) // 2
N_MOE = DEPTH // 2
SB_IN_W = 3 * TOK_W + X_W
GLA_XQ_OFF = 2 * GLA_DK + 2 * GLA_DV + GLA_GATE_RANK
GLA_IN_W = GLA_XQ_OFF + X_W

kernel_name = 'sb_gla_hybrid_decoder_step'


def rmsnorm(x, g):
    x32 = x.astype(jnp.float32)
    y = x32 * lax.rsqrt(jnp.mean(x32 * x32, axis=-1, keepdims=True) + RMS_EPS)
    return (y * g.astype(jnp.float32)).astype(x.dtype)


def split_heads(t, n_heads, d_head):
    return t.reshape(t.shape[0], t.shape[1], n_heads, d_head)


def stick_breaking(q, k, v, bias, q_pos, k_pos):
    z = jnp.einsum('bqhd,bkhd->bhqk', q, k).astype(jnp.float32) * (HEAD_DIM ** -0.5) \
        + bias.astype(jnp.float32)[None, :, None, None]
    mask = k_pos[None, :] < q_pos[:, None]
    log_1m = jnp.where(mask, jax.nn.log_sigmoid(-z), 0.0)
    between = lax.cumsum(log_1m, axis=3, reverse=True) - log_1m
    a = jnp.where(mask, jnp.exp(jax.nn.log_sigmoid(z) + between), 0.0)
    return jnp.einsum('bhqk,bkhd->bqhd', a.astype(v.dtype), v)


def sb_prompt(q, k, v, bias):
    B, S, H, D = q.shape
    k_pos = jnp.arange(S)
    def block(i):
        start = i * SB_BLOCK
        qb = lax.dynamic_slice_in_dim(q, start, SB_BLOCK, axis=1)
        return stick_breaking(qb, k, v, bias, start + jnp.arange(SB_BLOCK), k_pos)
    out = lax.map(block, jnp.arange(S // SB_BLOCK))
    return jnp.moveaxis(out, 0, 1).reshape(B, S, H, D)


def gla_chunked(q, k, v, log_a, state0):
    B, T, H, dk = q.shape
    dv = v.shape[-1]
    n = T // GLA_CHUNK
    def chunks(t):
        return jnp.moveaxis(t.reshape(B, n, GLA_CHUNK, H, t.shape[-1]), 1, 0)
    incl = jnp.tril(jnp.ones((GLA_CHUNK, GLA_CHUNK), dtype=bool))[None, :, :, None, None]
    def step(S, inp):
        qc, kc, vc, ac = inp
        b = jnp.cumsum(ac, axis=1)
        o_inter = jnp.einsum('bthk,bhkv->bthv', qc * jnp.exp(b), S)
        diff = b[:, :, None] - b[:, None, :]
        decay = jnp.exp(jnp.where(incl, diff, -jnp.inf))
        scores = jnp.einsum('bthk,bshk,btshk->bths', qc, kc, decay)
        o_intra = jnp.einsum('bths,bshv->bthv', scores, vc)
        b_last = b[:, -1]
        k_dec = kc * jnp.exp(b_last[:, None] - b)
        S_new = jnp.exp(b_last)[..., None] * S + jnp.einsum('bshk,bshv->bhkv', k_dec, vc)
        return S_new, o_inter + o_intra
    S, o = lax.scan(step, state0, (chunks(q), chunks(k), chunks(v), chunks(log_a)))
    return jnp.moveaxis(o, 0, 1).reshape(B, T, H, dv), S


def gla_recurrent(q, k, v, log_a, state0):
    def step(S, inp):
        qt, kt, vt, at = inp
        S = jnp.exp(at)[..., None] * S + kt[..., None] * vt[..., None, :]
        return S, jnp.einsum('bhk,bhkv->bhv', qt, S)
    xs = (jnp.moveaxis(q, 1, 0), jnp.moveaxis(k, 1, 0), jnp.moveaxis(v, 1, 0), jnp.moveaxis(log_a, 1, 0))
    S, o = lax.scan(step, state0, xs)
    return jnp.moveaxis(o, 0, 1), S


def gla_mixer(z, w_gate_up, b_gate, g_onorm, state0, chunked):
    B, T, _ = z.shape
    f32 = jnp.float32
    o1, o2 = GLA_DK, 2 * GLA_DK
    o3 = o2 + GLA_DV
    o4 = o3 + GLA_DV
    q = split_heads(z[..., :o1], GLA_HEADS, GLA_HDK).astype(f32) * (GLA_HDK ** -0.5)
    k = split_heads(z[..., o1:o2], GLA_HEADS, GLA_HDK).astype(f32)
    v = split_heads(z[..., o2:o3], GLA_HEADS, GLA_HDV).astype(f32)
    r = z[..., o3:o4]
    gate_logit = (z[..., o4:GLA_XQ_OFF] @ w_gate_up + b_gate).astype(f32)
    log_a = split_heads(jax.nn.log_sigmoid(gate_logit) / GLA_TAU, GLA_HEADS, GLA_HDK)
    s0 = state0.astype(f32)
    if chunked:
        o, S = gla_chunked(q, k, v, log_a, s0)
    else:
        o, S = gla_recurrent(q, k, v, log_a, s0)
    o = rmsnorm(o, g_onorm).reshape(B, T, GLA_DV).astype(z.dtype)
    return o * jax.nn.silu(r), S.astype(z.dtype)


def cross_attend(xq, mk, mv):
    B, T, _ = xq.shape
    q = split_heads(xq, X_HEADS, HEAD_DIM)
    s = jnp.einsum('bthd,bmhd->bhtm', q, mk.astype(q.dtype)).astype(jnp.float32) * (HEAD_DIM ** -0.5)
    p = jax.nn.softmax(s, axis=-1).astype(q.dtype)
    return jnp.einsum('bhtm,bmhd->bthd', p, mv.astype(q.dtype)).reshape(B, T, X_W)


def swiglu(h, w_gu, w_down):
    g, u = jnp.split(h @ w_gu, 2, axis=-1)
    return (jax.nn.silu(g) * u) @ w_down


def moe(h, w_router, w_gu, w_down):
    shp = h.shape
    t = h.reshape(-1, shp[-1])
    logits = (t @ w_router).astype(jnp.float32)
    top_logit, top_idx = lax.top_k(logits, TOP_K)
    gates = jax.nn.softmax(top_logit, axis=-1)
    combine = jnp.einsum('nk,nke->ne', gates, jax.nn.one_hot(top_idx, N_EXPERTS, dtype=jnp.float32)).astype(t.dtype)
    out = jnp.zeros_like(t)
    for e in range(N_EXPERTS):
        out = out + combine[:, e:e + 1] * swiglu(t, w_gu[e], w_down[e])
    return out.reshape(shp)


def setup_inputs(seed: int = 0) -> dict:
    key = jax.random.key(seed)
    ks = jax.random.split(key, 32)
    f32 = jnp.float32
    def nrm(k, shape, scale=1.0):
        return jax.random.normal(k, shape, f32) * scale
    def gain(k, shape):
        return 1.0 + 0.02 * jax.random.normal(k, shape, f32)
    n_pages = PAST_LEN // PAGE_SIZE
    n_used = DEC_BATCH * n_pages
    n_phys = n_used + n_used // 4
    page_table = jax.random.permutation(ks[0], n_phys)[:n_used].reshape(DEC_BATCH, n_pages).astype(jnp.int32)
    return {
        'x_prompt': nrm(ks[1], (BATCH, SEQ, D_MODEL)),
        'x_sample': nrm(ks[2], (DEC_BATCH, DEC_SEQ, D_MODEL)),
        'cache_sb_k': nrm(ks[3], (N_SB, n_phys, PAGE_SIZE, SB_HEADS, HEAD_DIM)),
        'cache_sb_v': nrm(ks[4], (N_SB, n_phys, PAGE_SIZE, SB_HEADS, HEAD_DIM)),
        'state_gla': nrm(ks[5], (N_GLA, DEC_BATCH, GLA_HEADS, GLA_HDK, GLA_HDV)),
        'cache_mem_k': nrm(ks[6], (DEPTH, DEC_BATCH, N_MEM, X_HEADS, HEAD_DIM)),
        'cache_mem_v': nrm(ks[7], (DEPTH, DEC_BATCH, N_MEM, X_HEADS, HEAD_DIM)),
        'page_table': page_table,
        'mem_prompt': nrm(ks[8], (BATCH, N_MEM, D_MODEL)),
        'g_mix': gain(ks[9], (DEPTH, D_MODEL)),
        'g_mem': gain(ks[10], (DEPTH, D_MODEL)),
        'w_mem_kv': nrm(ks[11], (DEPTH, D_MODEL, 2 * X_W), D_MODEL ** -0.5),
        'w_in_sb': nrm(ks[12], (N_SB, D_MODEL, SB_IN_W), D_MODEL ** -0.5),
        'b_sb': SB_BIAS_INIT + nrm(ks[25], (N_SB, SB_HEADS), 0.1),
        'w_in_gla': nrm(ks[13], (N_GLA, D_MODEL, GLA_IN_W), D_MODEL ** -0.5),
        'w_gate_up': nrm(ks[14], (N_GLA, GLA_GATE_RANK, GLA_DK), GLA_GATE_RANK ** -0.5),
        'b_gate': nrm(ks[15], (N_GLA, GLA_DK), 0.1),
        'g_gla_onorm': gain(ks[16], (N_GLA, GLA_HDV)),
        'w_out': nrm(ks[17], (DEPTH, MIX_W, D_MODEL), MIX_W ** -0.5),
        'g_ffn': gain(ks[18], (DEPTH, D_MODEL)),
        'w_ffn_gu': nrm(ks[19], (N_DENSE, D_MODEL, 2 * D_FF), D_MODEL ** -0.5),
        'w_ffn_down': nrm(ks[20], (N_DENSE, D_FF, D_MODEL), D_FF ** -0.5),
        'w_router': nrm(ks[21], (N_MOE, D_MODEL, N_EXPERTS), D_MODEL ** -0.5),
        'w_exp_gu': nrm(ks[22], (N_MOE, N_EXPERTS, D_MODEL, 2 * D_FF_EXPERT), D_MODEL ** -0.5),
        'w_exp_down': nrm(ks[23], (N_MOE, N_EXPERTS, D_FF_EXPERT, D_MODEL), D_FF_EXPERT ** -0.5),
        'g_final': gain(ks[24], (D_MODEL,)),
    }


def reference(x_prompt, x_sample, cache_sb_k, cache_sb_v, state_gla, cache_mem_k, cache_mem_v, page_table, mem_prompt,
              g_mix, g_mem, w_mem_kv, w_in_sb, b_sb, w_in_gla, w_gate_up, b_gate, g_gla_onorm, w_out,
              g_ffn, w_ffn_gu, w_ffn_down, w_router, w_exp_gu, w_exp_down, g_final):
    B, T, _ = x_prompt.shape
    Bs, Ts, _ = x_sample.shape
    past = page_table.shape[1] * PAGE_SIZE
    q_pos_s = past + jnp.arange(Ts)
    k_pos_s = jnp.arange(past + Ts)
    hp, hs = x_prompt, x_sample
    sb_kp, sb_vp, sb_ks, sb_vs = [], [], [], []
    gla_sp, gla_ss = [], []
    mem_kp, mem_vp = [], []
    for i in range(DEPTH):
        li = i // 2
        ap = rmsnorm(hp, g_mix[i])
        as_ = rmsnorm(hs, g_mix[i])
        mkv = rmsnorm(mem_prompt, g_mem[i]) @ w_mem_kv[i]
        mk_p = split_heads(mkv[..., :X_W], X_HEADS, HEAD_DIM)
        mv_p = split_heads(mkv[..., X_W:], X_HEADS, HEAD_DIM)
        mem_kp.append(mk_p)
        mem_vp.append(mv_p)
        if i % 2 == 0:
            zp = ap @ w_in_sb[li]
            zs = as_ @ w_in_sb[li]
            qp = split_heads(zp[..., :TOK_W], SB_HEADS, HEAD_DIM)
            kp = split_heads(zp[..., TOK_W:2 * TOK_W], SB_HEADS, HEAD_DIM)
            vp = split_heads(zp[..., 2 * TOK_W:3 * TOK_W], SB_HEADS, HEAD_DIM)
            qs = split_heads(zs[..., :TOK_W], SB_HEADS, HEAD_DIM)
            ks = split_heads(zs[..., TOK_W:2 * TOK_W], SB_HEADS, HEAD_DIM)
            vs = split_heads(zs[..., 2 * TOK_W:3 * TOK_W], SB_HEADS, HEAD_DIM)
            tok_p = sb_prompt(qp, kp, vp, b_sb[li]).reshape(B, T, TOK_W)
            past_k = cache_sb_k[li][page_table].reshape(Bs, past, SB_HEADS, HEAD_DIM).astype(ks.dtype)
            past_v = cache_sb_v[li][page_table].reshape(Bs, past, SB_HEADS, HEAD_DIM).astype(vs.dtype)
            k_all = jnp.concatenate([past_k, ks], axis=1)
            v_all = jnp.concatenate([past_v, vs], axis=1)
            tok_s = stick_breaking(qs, k_all, v_all, b_sb[li], q_pos_s, k_pos_s).reshape(Bs, Ts, TOK_W)
            sb_kp.append(kp)
            sb_vp.append(vp)
            sb_ks.append(ks)
            sb_vs.append(vs)
            xq_p, xq_s = zp[..., 3 * TOK_W:], zs[..., 3 * TOK_W:]
        else:
            zp = ap @ w_in_gla[li]
            zs = as_ @ w_in_gla[li]
            s0 = jnp.zeros((B, GLA_HEADS, GLA_HDK, GLA_HDV), jnp.float32)
            tok_p, S_p = gla_mixer(zp, w_gate_up[li], b_gate[li], g_gla_onorm[li], s0, True)
            tok_s, S_s = gla_mixer(zs, w_gate_up[li], b_gate[li], g_gla_onorm[li], state_gla[li], False)
            gla_sp.append(S_p)
            gla_ss.append(S_s)
            xq_p, xq_s = zp[..., GLA_XQ_OFF:], zs[..., GLA_XQ_OFF:]
        yp = jnp.concatenate([tok_p, cross_attend(xq_p, mk_p, mv_p)], axis=-1) @ w_out[i]
        ys = jnp.concatenate([tok_s, cross_attend(xq_s, cache_mem_k[i], cache_mem_v[i])], axis=-1) @ w_out[i]
        hp = hp + yp
        hs = hs + ys
        fp = rmsnorm(hp, g_ffn[i])
        fs = rmsnorm(hs, g_ffn[i])
        if i % 2 == 0:
            hp = hp + swiglu(fp, w_ffn_gu[li], w_ffn_down[li])
            hs = hs + swiglu(fs, w_ffn_gu[li], w_ffn_down[li])
        else:
            hp = hp + moe(fp, w_router[li], w_exp_gu[li], w_exp_down[li])
            hs = hs + moe(fs, w_router[li], w_exp_gu[li], w_exp_down[li])
    y_prompt = rmsnorm(hp, g_final)
    y_sample = rmsnorm(hs, g_final)
    sb_k_prompt = jnp.stack(sb_kp)
    sb_v_prompt = jnp.stack(sb_vp)
    sb_k_sample = jnp.stack(sb_ks)
    sb_v_sample = jnp.stack(sb_vs)
    gla_state_prompt = jnp.stack(gla_sp)
    gla_state_sample = jnp.stack(gla_ss)
    mem_k_prompt = jnp.stack(mem_kp)
    mem_v_prompt = jnp.stack(mem_vp)
    return (y_prompt, y_sample, sb_k_prompt, sb_v_prompt, sb_k_sample, sb_v_sample, gla_state_prompt, gla_state_sample, mem_k_prompt, mem_v_prompt)
```

```python
import functools

import jax
import jax.numpy as jnp
from jax import lax
from jax.experimental import pallas as pl
from jax.experimental.pallas import tpu as pltpu

F32 = jnp.float32
BF16 = jnp.bfloat16

D_MODEL = 1024
HEAD_DIM = 64
X_HEADS = 4
X_W = X_HEADS * HEAD_DIM
TOK_W = D_MODEL - X_W
SB_HEADS = TOK_W // HEAD_DIM
GLA_HEADS = 4
GLA_DV = TOK_W
GLA_DK = GLA_DV // 2
GLA_HDK = GLA_DK // GLA_HEADS
GLA_HDV = GLA_DV // GLA_HEADS
GLA_GATE_RANK = 16
GLA_TAU = 16.0
GLA_CHUNK = 64
N_EXPERTS = 8
RMS_EPS = 1e-6
PAGE_SIZE = 128

LANES = 128
SB_KB = 256
VMEM_LIMIT = 56 * 1024 * 1024


def _cparams(sem):
    return pltpu.CompilerParams(dimension_semantics=sem, vmem_limit_bytes=VMEM_LIMIT)


def _rms(x, g):
    return x * lax.rsqrt(jnp.mean(x * x, axis=-1, keepdims=True) + RMS_EPS) * g


def _dot(a, b):
    return jnp.dot(a, b, preferred_element_type=F32)


def _dot_nt(a, b):
    return lax.dot_general(a, b, (((1,), (1,)), ((), ())), preferred_element_type=F32)


def _dot_tn(a, b):
    return lax.dot_general(a, b, (((0,), (0,)), ((), ())), preferred_element_type=F32)


def _softplus(z):
    return jnp.maximum(z, 0.0) + jnp.log(1.0 + jnp.exp(-jnp.abs(z)))


def _tile(n, pref):
    t = min(n, pref)
    assert n % t == 0, (n, t)
    return t


def _norm_matmul_kernel(x_ref, g_ref, w_ref, o_ref):
    xn = _rms(x_ref[...], g_ref[...]).astype(BF16)
    o_ref[...] = _dot(xn, w_ref[...])


def norm_matmul(x, g, w_bf16, tm=512):
    n, d = x.shape
    m = w_bf16.shape[1]
    tm = _tile(n, tm)
    return pl.pallas_call(
        _norm_matmul_kernel,
        grid=(n // tm,),
        in_specs=[pl.BlockSpec((tm, d), lambda i: (i, 0)),
                  pl.BlockSpec((1, d), lambda i: (0, 0)),
                  pl.BlockSpec((d, m), lambda i: (0, 0))],
        out_specs=pl.BlockSpec((tm, m), lambda i: (i, 0)),
        out_shape=jax.ShapeDtypeStruct((n, m), F32),
        compiler_params=_cparams(("parallel",)),
        name="norm_matmul",
    )(x, g.reshape(1, d), w_bf16)


def _sb_inproj_kernel(x_ref, g_ref, w_ref, qh_ref, kt_ref, vh_ref, k_ref, v_ref, xq_ref):
    tm = x_ref.shape[0]
    xn = _rms(x_ref[...], g_ref[...]).astype(BF16)
    z = _dot(xn, w_ref[...])
    k_ref[...] = z[:, TOK_W:2 * TOK_W]
    v_ref[...] = z[:, 2 * TOK_W:3 * TOK_W]
    xq_ref[...] = z[:, 3 * TOK_W:]
    for h in range(SB_HEADS):
        lo = h * HEAD_DIM
        qh_ref[h] = (z[:, lo:lo + HEAD_DIM] * (HEAD_DIM ** -0.5)).astype(BF16)
        vh_ref[h] = z[:, 2 * TOK_W + lo:2 * TOK_W + lo + HEAD_DIM].astype(BF16)
    for p in range(SB_HEADS // 2):
        lo = TOK_W + p * LANES
        kt2 = z[:, lo:lo + LANES].T
        for hh in range(2):
            kt = kt2[hh * HEAD_DIM:(hh + 1) * HEAD_DIM]
            for c in range(tm // SB_KB):
                kt_ref[2 * p + hh, c] = kt[:, c * SB_KB:(c + 1) * SB_KB].astype(BF16)


def sb_inproj(x, g, w_bf16, tm=512):
    n, d = x.shape
    m = w_bf16.shape[1]
    tm = _tile(n, tm)
    nkb = tm // SB_KB
    return pl.pallas_call(
        _sb_inproj_kernel,
        grid=(n // tm,),
        in_specs=[pl.BlockSpec((tm, d), lambda i: (i, 0)),
                  pl.BlockSpec((1, d), lambda i: (0, 0)),
                  pl.BlockSpec((d, m), lambda i: (0, 0))],
        out_specs=[pl.BlockSpec((SB_HEADS, tm, HEAD_DIM), lambda i: (0, i, 0)),
                   pl.BlockSpec((SB_HEADS, nkb, HEAD_DIM, SB_KB), lambda i: (0, i, 0, 0)),
                   pl.BlockSpec((SB_HEADS, tm, HEAD_DIM), lambda i: (0, i, 0)),
                   pl.BlockSpec((tm, TOK_W), lambda i: (i, 0)),
                   pl.BlockSpec((tm, TOK_W), lambda i: (i, 0)),
                   pl.BlockSpec((tm, X_W), lambda i: (i, 0))],
        out_shape=[jax.ShapeDtypeStruct((SB_HEADS, n, HEAD_DIM), BF16),
                   jax.ShapeDtypeStruct((SB_HEADS, n // SB_KB, HEAD_DIM, SB_KB), BF16),
                   jax.ShapeDtypeStruct((SB_HEADS, n, HEAD_DIM), BF16),
                   jax.ShapeDtypeStruct((n, TOK_W), F32),
                   jax.ShapeDtypeStruct((n, TOK_W), F32),
                   jax.ShapeDtypeStruct((n, X_W), F32)],
        compiler_params=_cparams(("parallel",)),
        name="sb_inproj",
    )(x, g.reshape(1, d), w_bf16)


def _sb_block(q, kt, v, bias, tri, carry, acc, mask):
    z = _dot(q, kt) + bias
    sp = _softplus(z)
    if mask is not None:
        sp = jnp.where(mask, sp, 0.0)
    spb = sp.astype(BF16)
    local = _dot(spb, tri)
    a = jnp.exp(z - sp - local)
    if mask is not None:
        a = jnp.where(mask, a, 0.0)
    pv = _dot(a.astype(BF16), v)
    acc = acc + jnp.exp(-carry) * pv
    carry = carry + local[:, :1] + spb[:, :1].astype(F32)
    return carry, acc


def _sb_prompt_kernel(bias_ref, q_ref, kt_ref, v_ref, o_ref):
    hp = pl.program_id(1)
    i = pl.program_id(2)
    tq = q_ref.shape[1]
    row = lax.broadcasted_iota(jnp.int32, (SB_KB, SB_KB), 0)
    col = lax.broadcasted_iota(jnp.int32, (SB_KB, SB_KB), 1)
    tri = (row > col).astype(BF16)
    causal = col < row
    for hh in range(2):
        bias = bias_ref[2 * hp + hh]
        q = q_ref[hh]

        def step(j, carry, acc, mask, hh=hh, q=q, bias=bias):
            kt = kt_ref[hh, j]
            v = v_ref[hh, pl.ds(pl.multiple_of(j * SB_KB, SB_KB), SB_KB), :]
            return _sb_block(q, kt, v, bias, tri, carry, acc, mask)

        carry = jnp.zeros((tq, 1), F32)
        acc = jnp.zeros((tq, HEAD_DIM), F32)
        carry, acc = step(i, carry, acc, causal)

        def body(jj, c, step=step):
            return step(i - 1 - jj, c[0], c[1], None)

        carry, acc = lax.fori_loop(0, i, body, (carry, acc))
        o_ref[:, hh * HEAD_DIM:(hh + 1) * HEAD_DIM] = acc


def sb_prompt_attention(bias, qh, ktb, vh, batch):
    n = qh.shape[1]
    t = n // batch
    tq = SB_KB
    nq = t // tq
    return pl.pallas_call(
        _sb_prompt_kernel,
        grid=(batch, SB_HEADS // 2, nq),
        in_specs=[pl.BlockSpec(memory_space=pltpu.SMEM),
                  pl.BlockSpec((2, tq, HEAD_DIM), lambda b, h, i: (h, b * nq + i, 0)),
                  pl.BlockSpec((2, nq, HEAD_DIM, SB_KB), lambda b, h, i: (h, b, 0, 0)),
                  pl.BlockSpec((2, t, HEAD_DIM), lambda b, h, i: (h, b, 0))],
        out_specs=pl.BlockSpec((tq, 2 * HEAD_DIM), lambda b, h, i: (b * nq + i, h)),
        out_shape=jax.ShapeDtypeStruct((n, TOK_W), F32),
        compiler_params=_cparams(("parallel", "parallel", "arbitrary")),
        name="sb_prompt_attention",
    )(bias, qh, ktb, vh)


def _sb_sample_kernel(pt_ref, q_ref, bias_ref, k_ref, v_ref, o_ref, carry_sc, acc_sc):
    del pt_ref
    p = pl.program_id(1)
    hpad = acc_sc.shape[0]

    @pl.when(p == 0)
    def _():
        carry_sc[...] = jnp.zeros_like(carry_sc)
        acc_sc[...] = jnp.zeros_like(acc_sc)

    rows = lax.broadcasted_iota(jnp.int32, (hpad, TOK_W), 0)
    cols = lax.broadcasted_iota(jnp.int32, (hpad, TOK_W), 1)
    head_mask = (cols >= rows * HEAD_DIM) & (cols < (rows + 1) * HEAD_DIM)
    q = q_ref[0] * (HEAD_DIM ** -0.5)
    qbd = jnp.where(head_mask, q, 0.0).astype(BF16)
    r = lax.broadcasted_iota(jnp.int32, (PAGE_SIZE, PAGE_SIZE), 0)
    c = lax.broadcasted_iota(jnp.int32, (PAGE_SIZE, PAGE_SIZE), 1)
    tri = (r > c).astype(BF16)
    z = _dot_nt(qbd, k_ref[0].astype(BF16)) + bias_ref[...]
    sp = _softplus(z)
    spb = sp.astype(BF16)
    local = _dot(spb, tri)
    carry = carry_sc[:, :1]
    a = jnp.exp(z - sp - local - carry)
    acc_sc[...] += _dot(a.astype(BF16), v_ref[0].astype(BF16))
    carry_sc[...] += local[:, :1] + spb[:, :1].astype(F32)

    @pl.when(p == pl.num_programs(1) - 1)
    def _():
        o_ref[0] = jnp.sum(jnp.where(head_mask, acc_sc[...], 0.0), axis=0, keepdims=True)


def sb_sample_attention(q, bias, cache_k, cache_v, page_table):
    bs = q.shape[0]
    n_pages = page_table.shape[1]
    hpad = 16
    bias_b = jnp.zeros((hpad, PAGE_SIZE), F32).at[:SB_HEADS].set(
        jnp.broadcast_to(bias[:, None], (SB_HEADS, PAGE_SIZE)))
    pt = page_table.reshape(-1)

    def page_map(b, p, pt_ref):
        return (pt_ref[b * n_pages + (n_pages - 1 - p)], 0, 0)

    grid_spec = pltpu.PrefetchScalarGridSpec(
        num_scalar_prefetch=1,
        grid=(bs, n_pages),
        in_specs=[pl.BlockSpec((1, 1, TOK_W), lambda b, p, pt_ref: (b, 0, 0)),
                  pl.BlockSpec((hpad, PAGE_SIZE), lambda b, p, pt_ref: (0, 0)),
                  pl.BlockSpec((1, PAGE_SIZE, TOK_W), page_map),
                  pl.BlockSpec((1, PAGE_SIZE, TOK_W), page_map)],
        out_specs=pl.BlockSpec((1, 1, TOK_W), lambda b, p, pt_ref: (b, 0, 0)),
        scratch_shapes=[pltpu.VMEM((hpad, LANES), F32), pltpu.VMEM((hpad, TOK_W), F32)],
    )
    out = pl.pallas_call(
        _sb_sample_kernel,
        grid_spec=grid_spec,
        out_shape=jax.ShapeDtypeStruct((bs, 1, TOK_W), F32),
        compiler_params=_cparams(("parallel", "arbitrary")),
        name="sb_sample_attention",
    )(pt, q.reshape(bs, 1, TOK_W), bias_b, cache_k, cache_v)
    return out.reshape(bs, TOK_W)


def _softmax_rows(s):
    m = jnp.max(s, axis=-1, keepdims=True)
    e = jnp.exp(s - m)
    return e / jnp.sum(e, axis=-1, keepdims=True)


def _cross_prompt_kernel(q_ref, mkt_ref, mv_ref, o_ref):
    q = q_ref[...]
    for h in range(X_HEADS):
        lo = h * HEAD_DIM
        qh = q[:, lo:lo + HEAD_DIM].astype(BF16)
        s = _dot(qh, mkt_ref[0, h]) * (HEAD_DIM ** -0.5)
        p = _softmax_rows(s).astype(BF16)
        o_ref[:, lo:lo + HEAD_DIM] = _dot(p, mv_ref[0, h])


def cross_prompt(xq, mkt, mvh, batch, tq=512):
    n = xq.shape[0]
    t = n // batch
    tq = _tile(t, tq)
    nq = t // tq
    nm = mkt.shape[-1]
    return pl.pallas_call(
        _cross_prompt_kernel,
        grid=(batch, nq),
        in_specs=[pl.BlockSpec((tq, X_W), lambda b, i: (b * nq + i, 0)),
                  pl.BlockSpec((1, X_HEADS, HEAD_DIM, nm), lambda b, i: (b, 0, 0, 0)),
                  pl.BlockSpec((1, X_HEADS, nm, HEAD_DIM), lambda b, i: (b, 0, 0, 0))],
        out_specs=pl.BlockSpec((tq, X_W), lambda b, i: (b * nq + i, 0)),
        out_shape=jax.ShapeDtypeStruct((n, X_W), F32),
        compiler_params=_cparams(("parallel", "parallel")),
        name="cross_prompt",
    )(xq, mkt, mvh)


def _cross_sample_kernel(q_ref, mk_ref, mv_ref, o_ref):
    g = q_ref.shape[0]
    rows = lax.broadcasted_iota(jnp.int32, (8, X_W), 0)
    cols = lax.broadcasted_iota(jnp.int32, (8, X_W), 1)
    head_mask = (cols >= rows * HEAD_DIM) & (cols < (rows + 1) * HEAD_DIM)
    for j in range(g):
        q = q_ref[j:j + 1, :]
        qbd = jnp.where(head_mask, q, 0.0).astype(BF16)
        s = _dot_nt(qbd, mk_ref[j].astype(BF16)) * (HEAD_DIM ** -0.5)
        p = _softmax_rows(s).astype(BF16)
        o = _dot(p, mv_ref[j].astype(BF16))
        o_ref[j:j + 1, :] = jnp.sum(jnp.where(head_mask, o, 0.0), axis=0, keepdims=True)


def cross_sample(xq, mk, mv, g=8):
    bs = xq.shape[0]
    g = _tile(bs, g)
    nm = mk.shape[1]
    return pl.pallas_call(
        _cross_sample_kernel,
        grid=(bs // g,),
        in_specs=[pl.BlockSpec((g, X_W), lambda i: (i, 0)),
                  pl.BlockSpec((g, nm, X_W), lambda i: (i, 0, 0)),
                  pl.BlockSpec((g, nm, X_W), lambda i: (i, 0, 0))],
        out_specs=pl.BlockSpec((g, X_W), lambda i: (i, 0)),
        out_shape=jax.ShapeDtypeStruct((bs, X_W), F32),
        compiler_params=_cparams(("parallel",)),
        name="cross_sample",
    )(xq, mk, mv)


def _outproj_kernel(tok_ref, xa_ref, h_ref, wt_ref, wb_ref, o_ref):
    o_ref[...] = (h_ref[...] + _dot(tok_ref[...].astype(BF16), wt_ref[...])
                  + _dot(xa_ref[...].astype(BF16), wb_ref[...]))


def outproj(tok, xa, h, w_bf16, tm=512):
    n, d = h.shape
    tm = _tile(n, tm)
    wt, wb = w_bf16[:TOK_W], w_bf16[TOK_W:]
    return pl.pallas_call(
        _outproj_kernel,
        grid=(n // tm,),
        in_specs=[pl.BlockSpec((tm, TOK_W), lambda i: (i, 0)),
                  pl.BlockSpec((tm, X_W), lambda i: (i, 0)),
                  pl.BlockSpec((tm, d), lambda i: (i, 0)),
                  pl.BlockSpec((TOK_W, d), lambda i: (0, 0)),
                  pl.BlockSpec((X_W, d), lambda i: (0, 0))],
        out_specs=pl.BlockSpec((tm, d), lambda i: (i, 0)),
        out_shape=jax.ShapeDtypeStruct((n, d), F32),
        compiler_params=_cparams(("parallel",)),
        name="outproj",
    )(tok, xa, h, wt, wb)


def _ffn_kernel(x_ref, g_ref, wg_ref, wu_ref, wd_ref, o_ref, xn_sc, acc_sc):
    f = pl.program_id(1)

    @pl.when(f == 0)
    def _():
        xn_sc[...] = _rms(x_ref[...], g_ref[...]).astype(BF16)
        acc_sc[...] = jnp.zeros_like(acc_sc)

    xn = xn_sc[...]
    gate = _dot(xn, wg_ref[...])
    up = _dot(xn, wu_ref[...])
    act = (jax.nn.silu(gate) * up).astype(BF16)
    acc_sc[...] += _dot(act, wd_ref[...])

    @pl.when(f == pl.num_programs(1) - 1)
    def _():
        o_ref[...] = x_ref[...] + acc_sc[...]


def ffn(x, g, w_gu_bf16, w_down_bf16, tm=1024, tf=256):
    n, d = x.shape
    dff = w_down_bf16.shape[0]
    tm = _tile(n, tm)
    nf = dff // tf
    return pl.pallas_call(
        _ffn_kernel,
        grid=(n // tm, nf),
        in_specs=[pl.BlockSpec((tm, d), lambda i, f: (i, 0)),
                  pl.BlockSpec((1, d), lambda i, f: (0, 0)),
                  pl.BlockSpec((d, tf), lambda i, f: (0, f)),
                  pl.BlockSpec((d, tf), lambda i, f: (0, f + nf)),
                  pl.BlockSpec((tf, d), lambda i, f: (f, 0))],
        out_specs=pl.BlockSpec((tm, d), lambda i, f: (i, 0)),
        out_shape=jax.ShapeDtypeStruct((n, d), F32),
        scratch_shapes=[pltpu.VMEM((tm, d), BF16), pltpu.VMEM((tm, d), F32)],
        compiler_params=_cparams(("parallel", "arbitrary")),
        name="ffn",
    )(x, g.reshape(1, d), w_gu_bf16, w_gu_bf16, w_down_bf16)


def _split_bf16(x):
    hi = x.astype(BF16)
    lo = (x - hi.astype(F32)).astype(BF16)
    return hi, lo


def _router_kernel(x_ref, g_ref, w_ref, o_ref):
    xn = _rms(x_ref[...], g_ref[...])
    xh, xl = _split_bf16(xn)
    wh, wl = _split_bf16(w_ref[...])
    logits = _dot(xh, wh) + (_dot(xl, wh) + _dot(xh, wl))
    lane = lax.broadcasted_iota(jnp.int32, logits.shape, 1)
    neg = jnp.float32(-jnp.inf)
    logits = jnp.where(lane < N_EXPERTS, logits, neg)
    m1 = jnp.max(logits, axis=-1, keepdims=True)
    i1 = jnp.min(jnp.where(logits == m1, lane, LANES), axis=-1, keepdims=True)
    rest = jnp.where(lane == i1, neg, logits)
    m2 = jnp.max(rest, axis=-1, keepdims=True)
    i2 = jnp.min(jnp.where(rest == m2, lane, LANES), axis=-1, keepdims=True)
    e2 = jnp.exp(m2 - m1)
    den = 1.0 + e2
    o_ref[...] = jnp.where(lane == i1, 1.0 / den, 0.0) + jnp.where(lane == i2, e2 / den, 0.0)


def router(x, g, w_router, tm=512):
    n, d = x.shape
    tm = _tile(n, tm)
    w_pad = jnp.zeros((d, LANES), F32).at[:, :N_EXPERTS].set(w_router)
    return pl.pallas_call(
        _router_kernel,
        grid=(n // tm,),
        in_specs=[pl.BlockSpec((tm, d), lambda i: (i, 0)),
                  pl.BlockSpec((1, d), lambda i: (0, 0)),
                  pl.BlockSpec((d, LANES), lambda i: (0, 0))],
        out_specs=pl.BlockSpec((tm, LANES), lambda i: (i, 0)),
        out_shape=jax.ShapeDtypeStruct((n, LANES), F32),
        compiler_params=_cparams(("parallel",)),
        name="router",
    )(x, g.reshape(1, d), w_pad)


def _moe_kernel(x_ref, g_ref, comb_ref, wg_ref, wu_ref, wd_ref, gf_ref, o_ref, xn_sc, acc_sc):
    e = pl.program_id(1)
    f = pl.program_id(2)

    @pl.when((e == 0) & (f == 0))
    def _():
        xn_sc[...] = _rms(x_ref[...], g_ref[...]).astype(BF16)
        acc_sc[...] = jnp.zeros_like(acc_sc)

    xn = xn_sc[...]
    gate = _dot(xn, wg_ref[0])
    up = _dot(xn, wu_ref[0])
    act = (jax.nn.silu(gate) * up).astype(BF16)
    comb = comb_ref[...]
    lane = lax.broadcasted_iota(jnp.int32, comb.shape, 1)
    c_e = jnp.sum(jnp.where(lane == e, comb, 0.0), axis=-1, keepdims=True)
    acc_sc[...] += c_e * _dot(act, wd_ref[0])

    @pl.when((e == pl.num_programs(1) - 1) & (f == pl.num_programs(2) - 1))
    def _():
        o_ref[...] = _rms(x_ref[...] + acc_sc[...], gf_ref[...])


def moe_final(x, g, comb, w_gu_bf16, w_down_bf16, g_final, tm=1024, tf=512):
    n, d = x.shape
    ne, dff, _ = w_down_bf16.shape
    tm = _tile(n, tm)
    nf = dff // tf
    return pl.pallas_call(
        _moe_kernel,
        grid=(n // tm, ne, nf),
        in_specs=[pl.BlockSpec((tm, d), lambda i, e, f: (i, 0)),
                  pl.BlockSpec((1, d), lambda i, e, f: (0, 0)),
                  pl.BlockSpec((tm, LANES), lambda i, e, f: (i, 0)),
                  pl.BlockSpec((1, d, tf), lambda i, e, f: (e, 0, f)),
                  pl.BlockSpec((1, d, tf), lambda i, e, f: (e, 0, f + nf)),
                  pl.BlockSpec((1, tf, d), lambda i, e, f: (e, f, 0)),
                  pl.BlockSpec((1, d), lambda i, e, f: (0, 0))],
        out_specs=pl.BlockSpec((tm, d), lambda i, e, f: (i, 0)),
        out_shape=jax.ShapeDtypeStruct((n, d), F32),
        scratch_shapes=[pltpu.VMEM((tm, d), BF16), pltpu.VMEM((tm, d), F32)],
        compiler_params=_cparams(("parallel", "arbitrary", "arbitrary")),
        name="moe_final",
    )(x, g.reshape(1, d), comb, w_gu_bf16, w_gu_bf16, w_down_bf16, g_final.reshape(1, d))


GLA_R_OFF = 2 * GLA_DK + GLA_DV
GLA_XQ_OFF2 = GLA_R_OFF + GLA_DV
GLA_LR_OFF2 = GLA_XQ_OFF2 + X_W


def _head_of(col, width):
    return ((col >= width).astype(jnp.int32) + (col >= 2 * width).astype(jnp.int32)
            + (col >= 3 * width).astype(jnp.int32))


def _gla_log_decay(lr, wgu, bg):
    gl = _dot(lr.astype(BF16), wgu) + bg
    return -_softplus(-gl) / GLA_TAU


def _gla_out(o, r, gon):
    col = lax.broadcasted_iota(jnp.int32, (1, GLA_DV), 1)
    hv = _head_of(col, GLA_HDV)
    o2 = o * o
    inv = jnp.zeros_like(o)
    for h in range(GLA_HEADS):
        m = hv == h
        ms = jnp.sum(jnp.where(m, o2, 0.0), axis=-1, keepdims=True) * (1.0 / GLA_HDV)
        inv = jnp.where(m, lax.rsqrt(ms + RMS_EPS), inv)
    return (o * inv * gon) * jax.nn.silu(r)


def _gla_prompt_kernel(z_ref, wgu_ref, bg_ref, gon_ref, bd_ref, tok_ref, st_ref, st_sc):
    nb = z_ref.shape[0]
    gt = z_ref.shape[1]
    c = GLA_CHUNK

    @pl.when(pl.program_id(0) == 0)
    def _():
        st_sc[...] = jnp.zeros_like(st_sc)

    rr = lax.broadcasted_iota(jnp.int32, (c, c), 0)
    cc = lax.broadcasted_iota(jnp.int32, (c, c), 1)
    ltri = (cc <= rr).astype(BF16)
    r4 = lax.broadcasted_iota(jnp.int32, (c, GLA_HEADS * c), 0)
    c4 = lax.broadcasted_iota(jnp.int32, (c, GLA_HEADS * c), 1)
    intra_mask = (c4 & (c - 1)) <= r4
    hk = _head_of(lax.broadcasted_iota(jnp.int32, (1, GLA_DK), 1), GLA_HDK)
    hv = _head_of(lax.broadcasted_iota(jnp.int32, (1, GLA_DV), 1), GLA_HDV)
    wgu = wgu_ref[...]
    bg = bg_ref[...]
    gon = gon_ref[...]
    bd = bd_ref[...]

    for ci in range(gt // c):
        for b in range(nb):
            zc = z_ref[b, ci * c:(ci + 1) * c, :]
            q = zc[:, 0:GLA_DK] * (GLA_HDK ** -0.5)
            k = zc[:, GLA_DK:2 * GLA_DK]
            v = zc[:, 2 * GLA_DK:GLA_R_OFF]
            r = zc[:, GLA_R_OFF:GLA_XQ_OFF2]
            lr = zc[:, GLA_LR_OFF2:GLA_LR_OFF2 + GLA_GATE_RANK]
            la = _gla_log_decay(lr, wgu, bg)
            hi = la.astype(BF16)
            r1 = la - hi.astype(F32)
            mid = r1.astype(BF16)
            low = (r1 - mid.astype(F32)).astype(BF16)
            bc = _dot(ltri, hi) + (_dot(ltri, mid) + _dot(ltri, low))
            qe = (q * jnp.exp(bc)).astype(BF16)
            ke = (k * jnp.exp(-bc)).astype(BF16)
            vb = v.astype(BF16)
            ke_stack = jnp.concatenate(
                [jnp.where(hk == h, ke, jnp.zeros_like(ke)) for h in range(GLA_HEADS)], axis=0)
            v_stack = jnp.concatenate(
                [jnp.where(hv == h, vb, jnp.zeros_like(vb)) for h in range(GLA_HEADS)], axis=0)
            scores = jnp.where(intra_mask, _dot_nt(qe, ke_stack), 0.0)
            o_intra = _dot(scores.astype(BF16), v_stack)
            st = st_sc[b]
            o_inter = _dot_nt(qe, st.astype(BF16))
            tok_ref[b, ci * c:(ci + 1) * c, :] = _gla_out(o_inter + o_intra, r, gon)
            b_last = bc[c - 1:c, :]
            kd = (k * jnp.exp(b_last - bc)).astype(BF16)
            st_sc[b] = st * jnp.exp(b_last) + _dot_tn(vb, kd) * bd

    @pl.when(pl.program_id(0) == pl.num_programs(0) - 1)
    def _():
        st_ref[...] = st_sc[...]


def gla_prompt(z, w_gate_up, b_gate, g_onorm, gt=256):
    nb, t, zw = z.shape
    gt = _tile(t, gt)
    rv = lax.broadcasted_iota(jnp.int32, (GLA_DV, GLA_DK), 0) // GLA_HDV
    ck = lax.broadcasted_iota(jnp.int32, (GLA_DV, GLA_DK), 1) // GLA_HDK
    bd = (rv == ck).astype(F32)
    return pl.pallas_call(
        _gla_prompt_kernel,
        grid=(t // gt,),
        in_specs=[pl.BlockSpec((nb, gt, zw), lambda i: (0, i, 0)),
                  pl.BlockSpec((GLA_GATE_RANK, GLA_DK), lambda i: (0, 0)),
                  pl.BlockSpec((1, GLA_DK), lambda i: (0, 0)),
                  pl.BlockSpec((1, GLA_DV), lambda i: (0, 0)),
                  pl.BlockSpec((GLA_DV, GLA_DK), lambda i: (0, 0))],
        out_specs=[pl.BlockSpec((nb, gt, GLA_DV), lambda i: (0, i, 0)),
                   pl.BlockSpec((nb, GLA_DV, GLA_DK), lambda i: (0, 0, 0))],
        out_shape=[jax.ShapeDtypeStruct((nb, t, GLA_DV), F32),
                   jax.ShapeDtypeStruct((nb, GLA_DV, GLA_DK), F32)],
        scratch_shapes=[pltpu.VMEM((nb, GLA_DV, GLA_DK), F32)],
        compiler_params=_cparams(("arbitrary",)),
        name="gla_prompt",
    )(z, w_gate_up.astype(BF16), b_gate.reshape(1, GLA_DK), jnp.tile(g_onorm, GLA_HEADS).reshape(1, GLA_DV), bd)


def _gla_gate_kernel(z_ref, wgu_ref, bg_ref, q_ref, ea_ref):
    z = z_ref[...]
    la = _gla_log_decay(z[:, GLA_LR_OFF2:GLA_LR_OFF2 + GLA_GATE_RANK], wgu_ref[...], bg_ref[...])
    ea_ref[...] = jnp.exp(la)
    q_ref[...] = z[:, 0:GLA_DK] * (GLA_HDK ** -0.5)


def _gla_sample_kernel(q_ref, k_ref, ea_ref, z_ref, gon_ref, s0_ref, tok_ref, s_ref):
    g = q_ref.shape[0]
    for j in range(g):
        zr = z_ref[j]
        v = zr[:, 2 * GLA_DK:GLA_R_OFF]
        r = zr[:, GLA_R_OFF:GLA_XQ_OFF2]
        v_rows = jnp.concatenate(
            [jnp.broadcast_to(v[:, h * GLA_HDV:(h + 1) * GLA_HDV], (GLA_HDK, GLA_HDV))
             for h in range(GLA_HEADS)], axis=0)
        s_new = ea_ref[j] * s0_ref[j] + k_ref[j] * v_rows
        s_ref[j] = s_new
        qs = q_ref[j] * s_new
        o = jnp.concatenate(
            [jnp.sum(qs[h * GLA_HDK:(h + 1) * GLA_HDK], axis=0, keepdims=True) for h in range(GLA_HEADS)],
            axis=1)
        tok_ref[j] = _gla_out(o, r, gon_ref[...])


def gla_sample(z, w_gate_up, b_gate, g_onorm, state0, g=8):
    bs, zw = z.shape
    g = _tile(bs, g)
    q, ea = pl.pallas_call(
        _gla_gate_kernel,
        out_shape=[jax.ShapeDtypeStruct((bs, GLA_DK), F32), jax.ShapeDtypeStruct((bs, GLA_DK), F32)],
        name="gla_gate",
    )(z, w_gate_up.astype(BF16), b_gate.reshape(1, GLA_DK))
    k = z[:, GLA_DK:2 * GLA_DK]
    col = lambda a: a.reshape(bs, GLA_DK, 1)
    col_spec = pl.BlockSpec((g, GLA_DK, 1), lambda i: (i, 0, 0))
    tok, s_new = pl.pallas_call(
        _gla_sample_kernel,
        grid=(bs // g,),
        in_specs=[col_spec, col_spec, col_spec,
                  pl.BlockSpec((g, 1, zw), lambda i: (i, 0, 0)),
                  pl.BlockSpec((1, GLA_DV), lambda i: (0, 0)),
                  pl.BlockSpec((g, GLA_DK, GLA_HDV), lambda i: (i, 0, 0))],
        out_specs=[pl.BlockSpec((g, 1, GLA_DV), lambda i: (i, 0, 0)),
                   pl.BlockSpec((g, GLA_DK, GLA_HDV), lambda i: (i, 0, 0))],
        out_shape=[jax.ShapeDtypeStruct((bs, 1, GLA_DV), F32),
                   jax.ShapeDtypeStruct((bs, GLA_DK, GLA_HDV), F32)],
        compiler_params=_cparams(("parallel",)),
        name="gla_sample",
    )(col(q), col(k), col(ea), z.reshape(bs, 1, zw), jnp.tile(g_onorm, GLA_HEADS).reshape(1, GLA_DV), state0)
    return tok.reshape(bs, GLA_DV), s_new


def kernel(x_prompt, x_sample, cache_sb_k, cache_sb_v, state_gla, cache_mem_k, cache_mem_v, page_table, mem_prompt,
           g_mix, g_mem, w_mem_kv, w_in_sb, b_sb, w_in_gla, w_gate_up, b_gate, g_gla_onorm, w_out,
           g_ffn, w_ffn_gu, w_ffn_down, w_router, w_exp_gu, w_exp_down, g_final):
    B, T, D = x_prompt.shape
    Bs = x_sample.shape[0]
    n_mem = mem_prompt.shape[1]
    n_phys = cache_sb_k.shape[1]
    hp = x_prompt.reshape(B * T, D)
    hs = x_sample.reshape(Bs, D)
    mem = mem_prompt.reshape(B * n_mem, D)

    def mem_kv(i):
        mkv = norm_matmul(mem, g_mem[i], w_mem_kv[i].astype(BF16))
        mk = mkv[:, :X_W].reshape(B, n_mem, X_HEADS, HEAD_DIM)
        mv = mkv[:, X_W:].reshape(B, n_mem, X_HEADS, HEAD_DIM)
        mkt = jnp.transpose(mk, (0, 2, 3, 1)).astype(BF16)
        mvh = jnp.transpose(mv, (0, 2, 1, 3)).astype(BF16)
        return mk, mv, mkt, mvh

    def cross_s(xq, i):
        return cross_sample(xq, cache_mem_k[i].reshape(Bs, n_mem, X_W), cache_mem_v[i].reshape(Bs, n_mem, X_W))

    w_in = w_in_sb[0].astype(BF16)
    w_o = w_out[0].astype(BF16)
    mk0, mv0, mkt, mvh = mem_kv(0)
    qh, ktb, vh, k_p, v_p, xq_p = sb_inproj(hp, g_mix[0], w_in)
    tok_p = sb_prompt_attention(b_sb[0], qh, ktb, vh, B)
    xa_p = cross_prompt(xq_p, mkt, mvh, B)
    hp = outproj(tok_p, xa_p, hp, w_o)

    zs = norm_matmul(hs, g_mix[0], w_in)
    k_s, v_s, xq_s = zs[:, TOK_W:2 * TOK_W], zs[:, 2 * TOK_W:3 * TOK_W], zs[:, 3 * TOK_W:]
    tok_s = sb_sample_attention(zs[:, :TOK_W], b_sb[0],
                                cache_sb_k[0].reshape(n_phys, PAGE_SIZE, TOK_W),
                                cache_sb_v[0].reshape(n_phys, PAGE_SIZE, TOK_W), page_table)
    hs = outproj(tok_s, cross_s(xq_s, 0), hs, w_o)

    w_gu = w_ffn_gu[0].astype(BF16)
    w_dn = w_ffn_down[0].astype(BF16)
    hp = ffn(hp, g_ffn[0], w_gu, w_dn)
    hs = ffn(hs, g_ffn[0], w_gu, w_dn)

    w = w_in_gla[0]
    lr_off = GLA_R_OFF + GLA_DV
    w_in = jnp.concatenate([w[:, :lr_off], w[:, lr_off + GLA_GATE_RANK:], w[:, lr_off:lr_off + GLA_GATE_RANK]],
                           axis=1).astype(BF16)
    w_o = w_out[1].astype(BF16)
    mk1, mv1, mkt, mvh = mem_kv(1)
    zp = norm_matmul(hp, g_mix[1], w_in)
    tok_p, st_p = gla_prompt(zp.reshape(B, T, -1), w_gate_up[0], b_gate[0], g_gla_onorm[0])
    xa_p = cross_prompt(zp[:, GLA_XQ_OFF2:GLA_XQ_OFF2 + X_W], mkt, mvh, B)
    hp = outproj(tok_p.reshape(B * T, GLA_DV), xa_p, hp, w_o)

    zs = norm_matmul(hs, g_mix[1], w_in)
    tok_s, st_s = gla_sample(zs, w_gate_up[0], b_gate[0], g_gla_onorm[0],
                             state_gla[0].reshape(Bs, GLA_DK, GLA_HDV))
    hs = outproj(tok_s, cross_s(zs[:, GLA_XQ_OFF2:GLA_XQ_OFF2 + X_W], 1), hs, w_o)

    w_gu = w_exp_gu[0].astype(BF16)
    w_dn = w_exp_down[0].astype(BF16)
    y_p = moe_final(hp, g_ffn[1], router(hp, g_ffn[1], w_router[0]), w_gu, w_dn, g_final)
    y_s = moe_final(hs, g_ffn[1], router(hs, g_ffn[1], w_router[0]), w_gu, w_dn, g_final)

    st_p = jnp.stack([jnp.stack([st_p[b, h * GLA_HDV:(h + 1) * GLA_HDV, h * GLA_HDK:(h + 1) * GLA_HDK].T
                                 for h in range(GLA_HEADS)]) for b in range(B)])
    return (y_p.reshape(B, T, D), y_s.reshape(Bs, 1, D),
            k_p.reshape(1, B, T, SB_HEADS, HEAD_DIM), v_p.reshape(1, B, T, SB_HEADS, HEAD_DIM),
            k_s.reshape(1, Bs, 1, SB_HEADS, HEAD_DIM), v_s.reshape(1, Bs, 1, SB_HEADS, HEAD_DIM),
            st_p[None], st_s.reshape(1, Bs, GLA_HEADS, GLA_HDK, GLA_HDV),
            jnp.stack([mk0, mk1]), jnp.stack([mv0, mv1]))
```

```python
import functools

import jax
import jax.numpy as jnp
from jax import lax
from jax.experimental import pallas as pl
from jax.experimental.pallas import tpu as pltpu

F32 = jnp.float32
BF16 = jnp.bfloat16

D_MODEL = 1024
HEAD_DIM = 64
X_HEADS = 4
X_W = X_HEADS * HEAD_DIM
TOK_W = D_MODEL - X_W
SB_HEADS = TOK_W // HEAD_DIM
GLA_HEADS = 4
GLA_DV = TOK_W
GLA_DK = GLA_DV // 2
GLA_HDK = GLA_DK // GLA_HEADS
GLA_HDV = GLA_DV // GLA_HEADS
GLA_GATE_RANK = 16
GLA_TAU = 16.0
GLA_CHUNK = 64
N_EXPERTS = 8
RMS_EPS = 1e-6
PAGE_SIZE = 128

LANES = 128
SB_KB = 256
VMEM_LIMIT = 56 * 1024 * 1024


def _cparams(sem):
    return pltpu.CompilerParams(dimension_semantics=sem, vmem_limit_bytes=VMEM_LIMIT)


def _rms(x, g):
    return x * lax.rsqrt(jnp.mean(x * x, axis=-1, keepdims=True) + RMS_EPS) * g


def _dot(a, b):
    return jnp.dot(a, b, preferred_element_type=F32)


def _split_bf16(x):
    hi = x.astype(BF16)
    lo = (x - hi.astype(F32)).astype(BF16)
    return hi, lo


def _dot_split(a, b):
    ah, al = _split_bf16(a)
    bh, bl = _split_bf16(b)
    return _dot(ah, bh) + (_dot(al, bh) + _dot(ah, bl))


def _matmul(a, w):
    if w.dtype == BF16:
        return _dot(a.astype(BF16), w)
    return _dot_split(a, w)


def _dot_nt(a, b):
    return lax.dot_general(a, b, (((1,), (1,)), ((), ())), preferred_element_type=F32)


def _dot_tn(a, b):
    return lax.dot_general(a, b, (((0,), (0,)), ((), ())), preferred_element_type=F32)


def _softplus(z):
    return jnp.maximum(z, 0.0) + jnp.log(1.0 + jnp.exp(-jnp.abs(z)))


def _tile(n, pref):
    t = min(n, pref)
    assert n % t == 0, (n, t)
    return t


def _norm_matmul_kernel(x_ref, g_ref, w_ref, o_ref):
    o_ref[...] = _matmul(_rms(x_ref[...], g_ref[...]), w_ref[...])


def norm_matmul(x, g, w_bf16, tm=512):
    n, d = x.shape
    m = w_bf16.shape[1]
    tm = _tile(n, tm)
    return pl.pallas_call(
        _norm_matmul_kernel,
        grid=(n // tm,),
        in_specs=[pl.BlockSpec((tm, d), lambda i: (i, 0)),
                  pl.BlockSpec((1, d), lambda i: (0, 0)),
                  pl.BlockSpec((d, m), lambda i: (0, 0))],
        out_specs=pl.BlockSpec((tm, m), lambda i: (i, 0)),
        out_shape=jax.ShapeDtypeStruct((n, m), F32),
        compiler_params=_cparams(("parallel",)),
        name="norm_matmul",
    )(x, g.reshape(1, d), w_bf16)


def _sb_inproj_kernel(x_ref, g_ref, w_ref, qt_ref, kh_ref, vt_ref, k_ref, v_ref, xq_ref):
    tm = x_ref.shape[0]
    xn = _rms(x_ref[...], g_ref[...]).astype(BF16)
    z = _dot(xn, w_ref[...])
    k_ref[...] = z[:, TOK_W:2 * TOK_W]
    v_ref[...] = z[:, 2 * TOK_W:3 * TOK_W]
    xq_ref[...] = z[:, 3 * TOK_W:]
    for h in range(SB_HEADS):
        lo = TOK_W + h * HEAD_DIM
        kh_ref[h] = z[:, lo:lo + HEAD_DIM].astype(BF16)
    for p in range(SB_HEADS // 2):
        qt2 = (z[:, p * LANES:(p + 1) * LANES] * (HEAD_DIM ** -0.5)).T
        vt2 = z[:, 2 * TOK_W + p * LANES:2 * TOK_W + (p + 1) * LANES].T
        for hh in range(2):
            rows = slice(hh * HEAD_DIM, (hh + 1) * HEAD_DIM)
            for c in range(tm // SB_KB):
                cols = slice(c * SB_KB, (c + 1) * SB_KB)
                qt_ref[2 * p + hh, c] = qt2[rows, cols].astype(BF16)
                vt_ref[2 * p + hh, c] = vt2[rows, cols].astype(BF16)


def sb_inproj(x, g, w_bf16, tm=512):
    n, d = x.shape
    m = w_bf16.shape[1]
    tm = _tile(n, tm)
    nkb = tm // SB_KB
    t_spec = pl.BlockSpec((SB_HEADS, nkb, HEAD_DIM, SB_KB), lambda i: (0, i, 0, 0))
    t_shape = jax.ShapeDtypeStruct((SB_HEADS, n // SB_KB, HEAD_DIM, SB_KB), BF16)
    return pl.pallas_call(
        _sb_inproj_kernel,
        grid=(n // tm,),
        in_specs=[pl.BlockSpec((tm, d), lambda i: (i, 0)),
                  pl.BlockSpec((1, d), lambda i: (0, 0)),
                  pl.BlockSpec((d, m), lambda i: (0, 0))],
        out_specs=[t_spec,
                   pl.BlockSpec((SB_HEADS, tm, HEAD_DIM), lambda i: (0, i, 0)),
                   t_spec,
                   pl.BlockSpec((tm, TOK_W), lambda i: (i, 0)),
                   pl.BlockSpec((tm, TOK_W), lambda i: (i, 0)),
                   pl.BlockSpec((tm, X_W), lambda i: (i, 0))],
        out_shape=[t_shape,
                   jax.ShapeDtypeStruct((SB_HEADS, n, HEAD_DIM), BF16),
                   t_shape,
                   jax.ShapeDtypeStruct((n, TOK_W), F32),
                   jax.ShapeDtypeStruct((n, TOK_W), F32),
                   jax.ShapeDtypeStruct((n, X_W), F32)],
        compiler_params=_cparams(("parallel",)),
        name="sb_inproj",
    )(x, g.reshape(1, d), w_bf16)


LOG2E = 1.4426950408889634


def _sb_prompt_kernel(bias_ref, qt_ref, k_ref, vt_ref, o_ref, z_sc, t_sc, sp_sc, a_sc):
    hp = pl.program_id(1)
    i = pl.program_id(2)
    tq = qt_ref.shape[-1]
    row = lax.broadcasted_iota(jnp.int32, (SB_KB, SB_KB), 0)
    col = lax.broadcasted_iota(jnp.int32, (SB_KB, SB_KB), 1)
    tri = (col > row).astype(BF16)
    causal = row < col
    heads = range(2)

    def scores(j, hh):
        k = k_ref[hh, pl.ds(pl.multiple_of(j * SB_KB, SB_KB), SB_KB), :]
        return _dot(k, qt_ref[hh, 0]) + bias_ref[2 * hp + hh]

    def stage1(z, hh, slot, mask):
        e = jnp.exp2(jnp.abs(z) * (-LOG2E))
        sp = jnp.maximum(z, 0.0) + jnp.log(1.0 + e)
        t = z - sp
        if mask is not None:
            sp = jnp.where(mask, sp, 0.0)
            t = jnp.where(mask, t, -jnp.inf)
        t_sc[slot, hh] = t
        sp_sc[hh] = sp.astype(BF16)
        return sp[0:8, :].astype(BF16).astype(F32)

    def tick(j1, slot, state, with_stage1):
        heads8, ws, carries, accs = state
        j3 = jnp.minimum(j1 + 2, i)
        pvs = [_dot(vt_ref[hh, j3], a_sc[hh]) for hh in heads]
        locs = [_dot(tri, sp_sc[hh]) for hh in heads]
        heads8_next = heads8
        if with_stage1:
            zs_next = [scores(jnp.maximum(j1 - 1, 0), hh) for hh in heads]
            heads8_next = [stage1(z_sc[slot, hh], hh, 1 - slot, None) for hh in heads]
            for hh in heads:
                z_sc[1 - slot, hh] = zs_next[hh]
        ws_next, new_c, new_a = [], [], []
        for hh in heads:
            a_sc[hh] = jnp.exp(t_sc[slot, hh] - locs[hh]).astype(BF16)
            ws_next.append(jnp.exp(-carries[hh]))
            total = locs[hh][0:1, :] + heads8[hh][0:1, :]
            new_c.append(carries[hh] + jnp.broadcast_to(total, carries[hh].shape))
            new_a.append(accs[hh] + ws[hh][0:1, :] * pvs[hh])
        return heads8_next, ws_next, new_c, new_a

    heads8 = [stage1(scores(i, hh), hh, 0, causal) for hh in heads]
    for hh in heads:
        z_sc[0, hh] = scores(jnp.maximum(i - 1, 0), hh)
        a_sc[hh] = jnp.zeros((SB_KB, tq), BF16)
    zrow = [jnp.minimum(h8, 0.0) for h8 in heads8]
    state = (heads8, zrow, zrow, [jnp.zeros((HEAD_DIM, tq), F32) for _ in heads])
    state = lax.fori_loop(0, i, lambda jj, st: tick(i - 1 - jj, jj & 1, st, True), state)
    _, ws, _, accs = tick(-1, i & 1, state, False)
    accs = [accs[hh] + ws[hh][0:1, :] * _dot(vt_ref[hh, 0], a_sc[hh]) for hh in heads]
    o_ref[...] = jnp.concatenate(accs, axis=0).T


def sb_prompt_attention(bias, qt, kh, vt, batch):
    n = kh.shape[1]
    t = n // batch
    tq = SB_KB
    nq = t // tq
    return pl.pallas_call(
        _sb_prompt_kernel,
        grid=(batch, SB_HEADS // 2, nq),
        in_specs=[pl.BlockSpec(memory_space=pltpu.SMEM),
                  pl.BlockSpec((2, 1, HEAD_DIM, tq), lambda b, h, i: (h, b * nq + i, 0, 0)),
                  pl.BlockSpec((2, t, HEAD_DIM), lambda b, h, i: (h, b, 0)),
                  pl.BlockSpec((2, nq, HEAD_DIM, SB_KB), lambda b, h, i: (h, b, 0, 0))],
        out_specs=pl.BlockSpec((tq, 2 * HEAD_DIM), lambda b, h, i: (b * nq + i, h)),
        out_shape=jax.ShapeDtypeStruct((n, TOK_W), F32),
        scratch_shapes=[pltpu.VMEM((2, 2, SB_KB, tq), F32), pltpu.VMEM((2, 2, SB_KB, tq), F32),
                        pltpu.VMEM((2, SB_KB, tq), BF16), pltpu.VMEM((2, SB_KB, tq), BF16)],
        compiler_params=_cparams(("parallel", "parallel", "arbitrary")),
        name="sb_prompt_attention",
    )(bias, qt, kh, vt)


SB_PAGES_PER_STEP = 8


def _split3_bf16(x):
    hi = x.astype(BF16)
    r = x - hi.astype(F32)
    mid = r.astype(BF16)
    lo = (r - mid.astype(F32)).astype(BF16)
    return hi, mid, lo


def _sb_sample_kernel(pt_ref, q_ref, bias_ref, *refs):
    del pt_ref
    npg = (len(refs) - 3) // 2
    k_refs, v_refs = refs[:npg], refs[npg:2 * npg]
    o_ref, carry_sc, acc_sc = refs[2 * npg:]
    g = pl.program_id(1)
    hpad = carry_sc.shape[0]

    @pl.when(g == 0)
    def _():
        carry_sc[...] = jnp.zeros_like(carry_sc)
        acc_sc[...] = jnp.zeros_like(acc_sc)

    r = lax.broadcasted_iota(jnp.int32, (PAGE_SIZE, PAGE_SIZE), 0)
    c = lax.broadcasted_iota(jnp.int32, (PAGE_SIZE, PAGE_SIZE), 1)
    tri = (r > c).astype(BF16)
    qs = [q_ref[0, h] * (HEAD_DIM ** -0.5) for h in range(SB_HEADS)]
    pad = jnp.zeros((hpad - SB_HEADS, PAGE_SIZE), F32)
    carry = carry_sc[...]
    for p in range(npg):
        rows = [jnp.sum(k_refs[p][0, h] * qs[h], axis=0, keepdims=True) for h in range(SB_HEADS)]
        z = jnp.concatenate(rows + [pad], axis=0) + bias_ref[...]
        sp = _softplus(z)
        hi, mid, lo = _split3_bf16(sp)
        loc = _dot(hi, tri) + (_dot(mid, tri) + _dot(lo, tri))
        a = jnp.exp(z - sp - loc - carry)
        for h in range(SB_HEADS):
            acc_sc[h] += a[h:h + 1, :] * v_refs[p][0, h]
        carry = carry + jnp.broadcast_to(loc[:, 0:1] + sp[:, 0:1], carry.shape)
    carry_sc[...] = carry

    @pl.when(g == pl.num_programs(1) - 1)
    def _():
        for h in range(SB_HEADS):
            o_ref[0, h] = jnp.sum(acc_sc[h], axis=-1, keepdims=True)


def sb_sample_attention(q, bias, cache_k, cache_v, page_table):
    bs = q.shape[0]
    n_pages = page_table.shape[1]
    npg = _tile(n_pages, SB_PAGES_PER_STEP)
    hpad = 16
    bias_b = jnp.zeros((hpad, PAGE_SIZE), F32).at[:SB_HEADS].set(
        jnp.broadcast_to(bias[:, None], (SB_HEADS, PAGE_SIZE)))
    pt = page_table.reshape(-1)

    def page_spec(p):
        def index_map(b, g, pt_ref):
            return (pt_ref[b * n_pages + (n_pages - 1 - g * npg - p)], 0, 0, 0)
        return pl.BlockSpec((1, SB_HEADS, HEAD_DIM, PAGE_SIZE), index_map)

    qo_spec = pl.BlockSpec((1, SB_HEADS, HEAD_DIM, 1), lambda b, g, pt_ref: (b, 0, 0, 0))
    grid_spec = pltpu.PrefetchScalarGridSpec(
        num_scalar_prefetch=1,
        grid=(bs, n_pages // npg),
        in_specs=[qo_spec, pl.BlockSpec((hpad, PAGE_SIZE), lambda b, g, pt_ref: (0, 0))]
        + [page_spec(p) for p in range(npg)] * 2,
        out_specs=qo_spec,
        scratch_shapes=[pltpu.VMEM((hpad, PAGE_SIZE), F32), pltpu.VMEM((SB_HEADS, HEAD_DIM, PAGE_SIZE), F32)],
    )
    out = pl.pallas_call(
        _sb_sample_kernel,
        grid_spec=grid_spec,
        out_shape=jax.ShapeDtypeStruct((bs, SB_HEADS, HEAD_DIM, 1), F32),
        compiler_params=_cparams(("parallel", "arbitrary")),
        name="sb_sample_attention",
    )(pt, q.reshape(bs, SB_HEADS, HEAD_DIM, 1), bias_b, *([cache_k] * npg), *([cache_v] * npg))
    return out.reshape(bs, TOK_W)


def _softmax_rows(s):
    m = jnp.max(s, axis=-1, keepdims=True)
    e = jnp.exp(s - m)
    return e / jnp.sum(e, axis=-1, keepdims=True)


def _cross_prompt_kernel(q_ref, mkt_ref, mv_ref, o_ref):
    q = q_ref[...]
    for h in range(X_HEADS):
        lo = h * HEAD_DIM
        qh = q[:, lo:lo + HEAD_DIM].astype(BF16)
        s = _dot(qh, mkt_ref[0, h]) * (HEAD_DIM ** -0.5)
        p = _softmax_rows(s).astype(BF16)
        o_ref[:, lo:lo + HEAD_DIM] = _dot(p, mv_ref[0, h])


def cross_prompt(xq, mkt, mvh, batch, tq=512):
    n = xq.shape[0]
    t = n // batch
    tq = _tile(t, tq)
    nq = t // tq
    nm = mkt.shape[-1]
    return pl.pallas_call(
        _cross_prompt_kernel,
        grid=(batch, nq),
        in_specs=[pl.BlockSpec((tq, X_W), lambda b, i: (b * nq + i, 0)),
                  pl.BlockSpec((1, X_HEADS, HEAD_DIM, nm), lambda b, i: (b, 0, 0, 0)),
                  pl.BlockSpec((1, X_HEADS, nm, HEAD_DIM), lambda b, i: (b, 0, 0, 0))],
        out_specs=pl.BlockSpec((tq, X_W), lambda b, i: (b * nq + i, 0)),
        out_shape=jax.ShapeDtypeStruct((n, X_W), F32),
        compiler_params=_cparams(("parallel", "parallel")),
        name="cross_prompt",
    )(xq, mkt, mvh)


def _cross_sample_kernel(q_ref, mk_ref, mv_ref, o_ref):
    for j in range(q_ref.shape[0]):
        for h in range(X_HEADS):
            s = jnp.sum(mk_ref[j, h] * q_ref[j, h], axis=0, keepdims=True) * (HEAD_DIM ** -0.5)
            p = _softmax_rows(s)
            o_ref[j, h] = jnp.sum(mv_ref[j, h] * p, axis=-1, keepdims=True)


def cross_sample(xq, mk, mv, g=8):
    bs = xq.shape[0]
    g = _tile(bs, g)
    nm = mk.shape[-1]
    qo_spec = pl.BlockSpec((g, X_HEADS, HEAD_DIM, 1), lambda i: (i, 0, 0, 0))
    m_spec = pl.BlockSpec((g, X_HEADS, HEAD_DIM, nm), lambda i: (i, 0, 0, 0))
    out = pl.pallas_call(
        _cross_sample_kernel,
        grid=(bs // g,),
        in_specs=[qo_spec, m_spec, m_spec],
        out_specs=qo_spec,
        out_shape=jax.ShapeDtypeStruct((bs, X_HEADS, HEAD_DIM, 1), F32),
        compiler_params=_cparams(("parallel",)),
        name="cross_sample",
    )(xq.reshape(bs, X_HEADS, HEAD_DIM, 1), mk, mv)
    return out.reshape(bs, X_W)


def _outproj_kernel(tok_ref, xa_ref, h_ref, wt_ref, wb_ref, o_ref):
    o_ref[...] = h_ref[...] + _matmul(tok_ref[...], wt_ref[...]) + _matmul(xa_ref[...], wb_ref[...])


def outproj(tok, xa, h, w_bf16, tm=512):
    n, d = h.shape
    tm = _tile(n, tm)
    wt, wb = w_bf16[:TOK_W], w_bf16[TOK_W:]
    return pl.pallas_call(
        _outproj_kernel,
        grid=(n // tm,),
        in_specs=[pl.BlockSpec((tm, TOK_W), lambda i: (i, 0)),
                  pl.BlockSpec((tm, X_W), lambda i: (i, 0)),
                  pl.BlockSpec((tm, d), lambda i: (i, 0)),
                  pl.BlockSpec((TOK_W, d), lambda i: (0, 0)),
                  pl.BlockSpec((X_W, d), lambda i: (0, 0))],
        out_specs=pl.BlockSpec((tm, d), lambda i: (i, 0)),
        out_shape=jax.ShapeDtypeStruct((n, d), F32),
        compiler_params=_cparams(("parallel",)),
        name="outproj",
    )(tok, xa, h, wt, wb)


def _ffn_kernel(x_ref, g_ref, wg_ref, wu_ref, wd_ref, o_ref, xn_sc, acc_sc):
    f = pl.program_id(1)

    @pl.when(f == 0)
    def _():
        xn_sc[...] = _rms(x_ref[...], g_ref[...]).astype(xn_sc.dtype)
        acc_sc[...] = jnp.zeros_like(acc_sc)

    xn = xn_sc[...]
    gate = _matmul(xn, wg_ref[...])
    up = _matmul(xn, wu_ref[...])
    acc_sc[...] += _matmul(jax.nn.silu(gate) * up, wd_ref[...])

    @pl.when(f == pl.num_programs(1) - 1)
    def _():
        o_ref[...] = x_ref[...] + acc_sc[...]


def ffn(x, g, w_gu, w_down, tm=1024, tf=256):
    n, d = x.shape
    dff = w_down.shape[0]
    tm = _tile(n, tm)
    nf = dff // tf
    return pl.pallas_call(
        _ffn_kernel,
        grid=(n // tm, nf),
        in_specs=[pl.BlockSpec((tm, d), lambda i, f: (i, 0)),
                  pl.BlockSpec((1, d), lambda i, f: (0, 0)),
                  pl.BlockSpec((d, tf), lambda i, f: (0, f)),
                  pl.BlockSpec((d, tf), lambda i, f: (0, f + nf)),
                  pl.BlockSpec((tf, d), lambda i, f: (f, 0))],
        out_specs=pl.BlockSpec((tm, d), lambda i, f: (i, 0)),
        out_shape=jax.ShapeDtypeStruct((n, d), F32),
        scratch_shapes=[pltpu.VMEM((tm, d), w_gu.dtype), pltpu.VMEM((tm, d), F32)],
        compiler_params=_cparams(("parallel", "arbitrary")),
        name="ffn",
    )(x, g.reshape(1, d), w_gu, w_gu, w_down)


def _router_kernel(x_ref, g_ref, w_ref, o_ref, cnt_ref, cnt_sc):
    @pl.when(pl.program_id(0) == 0)
    def _():
        cnt_sc[...] = jnp.zeros_like(cnt_sc)

    xn = _rms(x_ref[...], g_ref[...])
    logits = _dot_split(xn, w_ref[...])
    lane = lax.broadcasted_iota(jnp.int32, logits.shape, 1)
    neg = jnp.float32(-jnp.inf)
    logits = jnp.where(lane < N_EXPERTS, logits, neg)
    m1 = jnp.max(logits, axis=-1, keepdims=True)
    i1 = jnp.min(jnp.where(logits == m1, lane, LANES), axis=-1, keepdims=True)
    rest = jnp.where(lane == i1, neg, logits)
    m2 = jnp.max(rest, axis=-1, keepdims=True)
    i2 = jnp.min(jnp.where(rest == m2, lane, LANES), axis=-1, keepdims=True)
    e2 = jnp.exp(m2 - m1)
    den = 1.0 + e2
    oh1 = lane == i1
    oh2 = lane == i2
    cnt = (oh1 | oh2).astype(BF16)
    tm = cnt.shape[0]
    r = lax.broadcasted_iota(jnp.int32, (tm, tm), 0)
    c = lax.broadcasted_iota(jnp.int32, (tm, tm), 1)
    before = _dot((c < r).astype(BF16), cnt) + cnt_sc[...]
    r1 = jnp.sum(jnp.where(oh1, before, 0.0), axis=-1, keepdims=True)
    r2 = jnp.sum(jnp.where(oh2, before, 0.0), axis=-1, keepdims=True)
    cnt_sc[...] = before[tm - 1:tm, :] + cnt[tm - 1:tm, :].astype(F32)
    cnt_ref[...] = cnt_sc[...]
    info = jnp.zeros(logits.shape, F32)
    for k, val in enumerate((i1.astype(F32), i2.astype(F32), 1.0 / den, e2 / den, r1, r2)):
        info = jnp.where(lane == k, val, info)
    o_ref[...] = info


R_E1, R_E2, R_G1, R_G2, R_R1, R_R2 = range(6)


def router(x, g, w_router, tm=512):
    n, d = x.shape
    tm = _tile(n, tm)
    w_pad = jnp.zeros((d, LANES), F32).at[:, :N_EXPERTS].set(w_router)
    return pl.pallas_call(
        _router_kernel,
        grid=(n // tm,),
        in_specs=[pl.BlockSpec((tm, d), lambda i: (i, 0)),
                  pl.BlockSpec((1, d), lambda i: (0, 0)),
                  pl.BlockSpec((d, LANES), lambda i: (0, 0))],
        out_specs=[pl.BlockSpec((tm, LANES), lambda i: (i, 0)),
                   pl.BlockSpec((1, LANES), lambda i: (0, 0))],
        out_shape=[jax.ShapeDtypeStruct((n, LANES), F32), jax.ShapeDtypeStruct((1, LANES), F32)],
        scratch_shapes=[pltpu.VMEM((1, LANES), F32)],
        compiler_params=_cparams(("arbitrary",)),
        name="router",
    )(x, g.reshape(1, d), w_pad)


def _row_copy(src_hbm, src_row, dst_ref, dst_row, sem):
    return pltpu.make_async_copy(src_hbm.at[pl.ds(src_row, 1)], dst_ref.at[pl.ds(dst_row, 1)], sem)


def _gather_rows_kernel(src_ref, x_hbm, o_ref, sem):
    tg = o_ref.shape[0]
    base = pl.program_id(0) * tg

    def start(r, carry):
        _row_copy(x_hbm, src_ref[base + r], o_ref, r, sem).start()
        return carry

    def wait(r, carry):
        _row_copy(x_hbm, 0, o_ref, r, sem).wait()
        return carry

    lax.fori_loop(0, tg, start, 0, unroll=8)
    lax.fori_loop(0, tg, wait, 0, unroll=8)


def gather_rows(x, src, tg):
    n_slots = src.shape[0]
    d = x.shape[1]
    grid_spec = pltpu.PrefetchScalarGridSpec(
        num_scalar_prefetch=1,
        grid=(n_slots // tg,),
        in_specs=[pl.BlockSpec(memory_space=pl.ANY)],
        out_specs=pl.BlockSpec((tg, d), lambda t, src_ref: (t, 0)),
        scratch_shapes=[pltpu.SemaphoreType.DMA(())],
    )
    return pl.pallas_call(
        _gather_rows_kernel,
        grid_spec=grid_spec,
        out_shape=jax.ShapeDtypeStruct((n_slots, d), x.dtype),
        compiler_params=_cparams(("arbitrary",)),
        name="gather_rows",
    )(src, x)


def _expert_kernel(te_ref, nv_ref, x_ref, g_ref, wg_ref, wu_ref, wd_ref, o_ref, xn_sc, acc_sc):
    del te_ref
    t = pl.program_id(0)
    f = pl.program_id(1)
    last = pl.num_programs(1) - 1

    @pl.when(t < nv_ref[0])
    def _():
        @pl.when(f == 0)
        def _():
            xn_sc[...] = _rms(x_ref[...], g_ref[...]).astype(BF16)
            acc_sc[...] = jnp.zeros_like(acc_sc)

        xn = xn_sc[...]
        gate = _dot(xn, wg_ref[0])
        up = _dot(xn, wu_ref[0])
        act = (jax.nn.silu(gate) * up).astype(BF16)
        acc_sc[...] += _dot(act, wd_ref[0])

        @pl.when(f == last)
        def _():
            o_ref[...] = acc_sc[...]

    @pl.when((t >= nv_ref[0]) & (f == last))
    def _():
        o_ref[...] = jnp.zeros_like(o_ref)


def expert_swiglu(xs, g, tile_expert, n_valid, w_gu_bf16, w_down_bf16, tm, tf=512):
    n_slots, d = xs.shape
    dff = w_down_bf16.shape[1]
    nf = dff // tf

    def wmap(off):
        def index_map(t, f, te_ref, nv_ref):
            return (te_ref[t], 0, jnp.where(t < nv_ref[0], f, nf - 1) + off)
        return index_map

    def dmap(t, f, te_ref, nv_ref):
        return (te_ref[t], jnp.where(t < nv_ref[0], f, nf - 1), 0)

    grid_spec = pltpu.PrefetchScalarGridSpec(
        num_scalar_prefetch=2,
        grid=(n_slots // tm, nf),
        in_specs=[pl.BlockSpec((tm, d), lambda t, f, te_ref, nv_ref: (t, 0)),
                  pl.BlockSpec((1, d), lambda t, f, te_ref, nv_ref: (0, 0)),
                  pl.BlockSpec((1, d, tf), wmap(0)),
                  pl.BlockSpec((1, d, tf), wmap(nf)),
                  pl.BlockSpec((1, tf, d), dmap)],
        out_specs=pl.BlockSpec((tm, d), lambda t, f, te_ref, nv_ref: (t, 0)),
        scratch_shapes=[pltpu.VMEM((tm, d), BF16), pltpu.VMEM((tm, d), F32)],
    )
    return pl.pallas_call(
        _expert_kernel,
        grid_spec=grid_spec,
        out_shape=jax.ShapeDtypeStruct((n_slots, d), F32),
        compiler_params=_cparams(("arbitrary", "arbitrary")),
        name="expert_swiglu",
    )(tile_expert, n_valid, xs, g.reshape(1, d), w_gu_bf16, w_gu_bf16, w_down_bf16)


def _combine_kernel(s1_ref, s2_ref, x_ref, info_ref, gf_ref, ys_hbm, o_ref, buf, sem):
    tc = x_ref.shape[0]
    base = pl.program_id(0) * tc

    def start(r, carry):
        _row_copy(ys_hbm, s1_ref[base + r], buf.at[0], r, sem).start()
        _row_copy(ys_hbm, s2_ref[base + r], buf.at[1], r, sem).start()
        return carry

    def wait(r, carry):
        _row_copy(ys_hbm, 0, buf.at[0], r, sem).wait()
        _row_copy(ys_hbm, 0, buf.at[1], r, sem).wait()
        return carry

    lax.fori_loop(0, tc, start, 0, unroll=8)
    lax.fori_loop(0, tc, wait, 0, unroll=8)
    info = info_ref[...]
    g1 = info[:, R_G1:R_G1 + 1]
    g2 = info[:, R_G2:R_G2 + 1]
    o_ref[...] = _rms(x_ref[...] + (g1 * buf[0] + g2 * buf[1]), gf_ref[...])


def combine_final(x, info, slot1, slot2, ys, g_final, tc=256):
    n, d = x.shape
    tc = _tile(n, tc)
    grid_spec = pltpu.PrefetchScalarGridSpec(
        num_scalar_prefetch=2,
        grid=(n // tc,),
        in_specs=[pl.BlockSpec((tc, d), lambda i, s1, s2: (i, 0)),
                  pl.BlockSpec((tc, LANES), lambda i, s1, s2: (i, 0)),
                  pl.BlockSpec((1, d), lambda i, s1, s2: (0, 0)),
                  pl.BlockSpec(memory_space=pl.ANY)],
        out_specs=pl.BlockSpec((tc, d), lambda i, s1, s2: (i, 0)),
        scratch_shapes=[pltpu.VMEM((2, tc, d), F32), pltpu.SemaphoreType.DMA(())],
    )
    return pl.pallas_call(
        _combine_kernel,
        grid_spec=grid_spec,
        out_shape=jax.ShapeDtypeStruct((n, d), F32),
        compiler_params=_cparams(("arbitrary",)),
        name="combine_final",
    )(slot1, slot2, x, info, g_final.reshape(1, d), ys)


def moe_final(x, g, w_router, w_gu_bf16, w_down_bf16, g_final, tm):
    n, d = x.shape
    info, counts = router(x, g, w_router)
    counts = counts[0, :N_EXPERTS].astype(jnp.int32)
    padded = ((counts + tm - 1) // tm) * tm
    ends = jnp.cumsum(padded)
    offs = ends - padded
    n_slots = (2 * n // tm + N_EXPERTS) * tm
    e1 = info[:, R_E1].astype(jnp.int32)
    e2 = info[:, R_E2].astype(jnp.int32)
    slot1 = offs[e1] + info[:, R_R1].astype(jnp.int32)
    slot2 = offs[e2] + info[:, R_R2].astype(jnp.int32)
    tok = jnp.arange(n, dtype=jnp.int32)
    src = jnp.zeros((n_slots,), jnp.int32).at[slot1].set(tok).at[slot2].set(tok)
    tile_start = jnp.arange(n_slots // tm, dtype=jnp.int32) * tm
    n_valid = (ends[-1] // tm).reshape(1)
    tile_expert = jnp.minimum(jnp.sum(tile_start[:, None] >= ends[None, :], axis=1), N_EXPERTS - 1).astype(jnp.int32)
    tile_expert = jnp.where(tile_start < ends[-1], tile_expert, tile_expert[jnp.maximum(n_valid[0] - 1, 0)])
    xs = gather_rows(x, src, tm)
    ys = expert_swiglu(xs, g, tile_expert, n_valid, w_gu_bf16, w_down_bf16, tm)
    return combine_final(x, info, slot1, slot2, ys, g_final)


GLA_R_OFF = 2 * GLA_DK + GLA_DV
GLA_XQ_OFF2 = GLA_R_OFF + GLA_DV
GLA_LR_OFF2 = GLA_XQ_OFF2 + X_W


def _head_of(col, width):
    return ((col >= width).astype(jnp.int32) + (col >= 2 * width).astype(jnp.int32)
            + (col >= 3 * width).astype(jnp.int32))


def _gla_log_decay(lr, wgu, bg):
    gl = _matmul(lr, wgu) + bg
    return -_softplus(-gl) / GLA_TAU


def _gla_out(o, r, gon):
    col = lax.broadcasted_iota(jnp.int32, (1, GLA_DV), 1)
    hv = _head_of(col, GLA_HDV)
    o2 = o * o
    inv = jnp.zeros_like(o)
    for h in range(GLA_HEADS):
        m = hv == h
        ms = jnp.sum(jnp.where(m, o2, 0.0), axis=-1, keepdims=True) * (1.0 / GLA_HDV)
        inv = jnp.where(m, lax.rsqrt(ms + RMS_EPS), inv)
    return (o * inv * gon) * jax.nn.silu(r)


def _gla_prompt_kernel(z_ref, wgu_ref, bg_ref, gon_ref, bd_ref, tok_ref, st_ref, st_sc):
    nb = z_ref.shape[0]
    gt = z_ref.shape[1]
    c = GLA_CHUNK

    @pl.when(pl.program_id(0) == 0)
    def _():
        st_sc[...] = jnp.zeros_like(st_sc)

    rr = lax.broadcasted_iota(jnp.int32, (c, c), 0)
    cc = lax.broadcasted_iota(jnp.int32, (c, c), 1)
    ltri = (cc <= rr).astype(BF16)
    r4 = lax.broadcasted_iota(jnp.int32, (c, GLA_HEADS * c), 0)
    c4 = lax.broadcasted_iota(jnp.int32, (c, GLA_HEADS * c), 1)
    intra_mask = (c4 & (c - 1)) <= r4
    hk = _head_of(lax.broadcasted_iota(jnp.int32, (1, GLA_DK), 1), GLA_HDK)
    hv = _head_of(lax.broadcasted_iota(jnp.int32, (1, GLA_DV), 1), GLA_HDV)
    wgu = wgu_ref[...]
    bg = bg_ref[...]
    gon = gon_ref[...]
    bd = bd_ref[...]

    for ci in range(gt // c):
        for b in range(nb):
            zc = z_ref[b, ci * c:(ci + 1) * c, :]
            q = zc[:, 0:GLA_DK] * (GLA_HDK ** -0.5)
            k = zc[:, GLA_DK:2 * GLA_DK]
            v = zc[:, 2 * GLA_DK:GLA_R_OFF]
            r = zc[:, GLA_R_OFF:GLA_XQ_OFF2]
            lr = zc[:, GLA_LR_OFF2:GLA_LR_OFF2 + GLA_GATE_RANK]
            la = _gla_log_decay(lr, wgu, bg)
            hi = la.astype(BF16)
            r1 = la - hi.astype(F32)
            mid = r1.astype(BF16)
            low = (r1 - mid.astype(F32)).astype(BF16)
            bc = _dot(ltri, hi) + (_dot(ltri, mid) + _dot(ltri, low))
            qe = (q * jnp.exp(bc)).astype(BF16)
            ke = (k * jnp.exp(-bc)).astype(BF16)
            vb = v.astype(BF16)
            ke_stack = jnp.concatenate(
                [jnp.where(hk == h, ke, jnp.zeros_like(ke)) for h in range(GLA_HEADS)], axis=0)
            v_stack = jnp.concatenate(
                [jnp.where(hv == h, vb, jnp.zeros_like(vb)) for h in range(GLA_HEADS)], axis=0)
            scores = jnp.where(intra_mask, _dot_nt(qe, ke_stack), 0.0)
            o_intra = _dot(scores.astype(BF16), v_stack)
            st = st_sc[b]
            o_inter = _dot_nt(qe, st.astype(BF16))
            tok_ref[b, ci * c:(ci + 1) * c, :] = _gla_out(o_inter + o_intra, r, gon)
            b_last = bc[c - 1:c, :]
            kd = (k * jnp.exp(b_last - bc)).astype(BF16)
            st_sc[b] = st * jnp.exp(b_last) + _dot_tn(vb, kd) * bd

    @pl.when(pl.program_id(0) == pl.num_programs(0) - 1)
    def _():
        st_ref[...] = st_sc[...]


def gla_prompt(z, w_gate_up, b_gate, g_onorm, gt=256):
    nb, t, zw = z.shape
    gt = _tile(t, gt)
    rv = lax.broadcasted_iota(jnp.int32, (GLA_DV, GLA_DK), 0) // GLA_HDV
    ck = lax.broadcasted_iota(jnp.int32, (GLA_DV, GLA_DK), 1) // GLA_HDK
    bd = (rv == ck).astype(F32)
    return pl.pallas_call(
        _gla_prompt_kernel,
        grid=(t // gt,),
        in_specs=[pl.BlockSpec((nb, gt, zw), lambda i: (0, i, 0)),
                  pl.BlockSpec((GLA_GATE_RANK, GLA_DK), lambda i: (0, 0)),
                  pl.BlockSpec((1, GLA_DK), lambda i: (0, 0)),
                  pl.BlockSpec((1, GLA_DV), lambda i: (0, 0)),
                  pl.BlockSpec((GLA_DV, GLA_DK), lambda i: (0, 0))],
        out_specs=[pl.BlockSpec((nb, gt, GLA_DV), lambda i: (0, i, 0)),
                   pl.BlockSpec((nb, GLA_DV, GLA_DK), lambda i: (0, 0, 0))],
        out_shape=[jax.ShapeDtypeStruct((nb, t, GLA_DV), F32),
                   jax.ShapeDtypeStruct((nb, GLA_DV, GLA_DK), F32)],
        scratch_shapes=[pltpu.VMEM((nb, GLA_DV, GLA_DK), F32)],
        compiler_params=_cparams(("arbitrary",)),
        name="gla_prompt",
    )(z, w_gate_up.astype(BF16), b_gate.reshape(1, GLA_DK), jnp.tile(g_onorm, GLA_HEADS).reshape(1, GLA_DV), bd)


def _gla_gate_kernel(z_ref, wgu_ref, bg_ref, q_ref, ea_ref):
    z = z_ref[...]
    la = _gla_log_decay(z[:, GLA_LR_OFF2:GLA_LR_OFF2 + GLA_GATE_RANK], wgu_ref[...], bg_ref[...])
    ea_ref[...] = jnp.exp(la)
    q_ref[...] = z[:, 0:GLA_DK] * (GLA_HDK ** -0.5)


def _gla_sample_kernel(q_ref, k_ref, ea_ref, z_ref, gon_ref, s0_ref, tok_ref, s_ref):
    g = q_ref.shape[0]
    for j in range(g):
        zr = z_ref[j]
        v = zr[:, 2 * GLA_DK:GLA_R_OFF]
        r = zr[:, GLA_R_OFF:GLA_XQ_OFF2]
        v_rows = jnp.concatenate(
            [jnp.broadcast_to(v[:, h * GLA_HDV:(h + 1) * GLA_HDV], (GLA_HDK, GLA_HDV))
             for h in range(GLA_HEADS)], axis=0)
        s_new = ea_ref[j] * s0_ref[j] + k_ref[j] * v_rows
        s_ref[j] = s_new
        qs = q_ref[j] * s_new
        o = jnp.concatenate(
            [jnp.sum(qs[h * GLA_HDK:(h + 1) * GLA_HDK], axis=0, keepdims=True) for h in range(GLA_HEADS)],
            axis=1)
        tok_ref[j] = _gla_out(o, r, gon_ref[...])


def gla_sample(z, w_gate_up, b_gate, g_onorm, state0, g=8):
    bs, zw = z.shape
    g = _tile(bs, g)
    q, ea = pl.pallas_call(
        _gla_gate_kernel,
        out_shape=[jax.ShapeDtypeStruct((bs, GLA_DK), F32), jax.ShapeDtypeStruct((bs, GLA_DK), F32)],
        name="gla_gate",
    )(z, w_gate_up, b_gate.reshape(1, GLA_DK))
    k = z[:, GLA_DK:2 * GLA_DK]
    col = lambda a: a.reshape(bs, GLA_DK, 1)
    col_spec = pl.BlockSpec((g, GLA_DK, 1), lambda i: (i, 0, 0))
    tok, s_new = pl.pallas_call(
        _gla_sample_kernel,
        grid=(bs // g,),
        in_specs=[col_spec, col_spec, col_spec,
                  pl.BlockSpec((g, 1, zw), lambda i: (i, 0, 0)),
                  pl.BlockSpec((1, GLA_DV), lambda i: (0, 0)),
                  pl.BlockSpec((g, GLA_DK, GLA_HDV), lambda i: (i, 0, 0))],
        out_specs=[pl.BlockSpec((g, 1, GLA_DV), lambda i: (i, 0, 0)),
                   pl.BlockSpec((g, GLA_DK, GLA_HDV), lambda i: (i, 0, 0))],
        out_shape=[jax.ShapeDtypeStruct((bs, 1, GLA_DV), F32),
                   jax.ShapeDtypeStruct((bs, GLA_DK, GLA_HDV), F32)],
        compiler_params=_cparams(("parallel",)),
        name="gla_sample",
    )(col(q), col(k), col(ea), z.reshape(bs, 1, zw), jnp.tile(g_onorm, GLA_HEADS).reshape(1, GLA_DV), state0)
    return tok.reshape(bs, GLA_DV), s_new


def kernel(x_prompt, x_sample, cache_sb_k, cache_sb_v, state_gla, cache_mem_k, cache_mem_v, page_table, mem_prompt,
           g_mix, g_mem, w_mem_kv, w_in_sb, b_sb, w_in_gla, w_gate_up, b_gate, g_gla_onorm, w_out,
           g_ffn, w_ffn_gu, w_ffn_down, w_router, w_exp_gu, w_exp_down, g_final):
    B, T, D = x_prompt.shape
    Bs = x_sample.shape[0]
    n_mem = mem_prompt.shape[1]
    n_phys = cache_sb_k.shape[1]
    hp = x_prompt.reshape(B * T, D)
    hs = x_sample.reshape(Bs, D)
    mem = mem_prompt.reshape(B * n_mem, D)

    def mem_kv(i):
        mkv = norm_matmul(mem, g_mem[i], w_mem_kv[i].astype(BF16))
        mk = mkv[:, :X_W].reshape(B, n_mem, X_HEADS, HEAD_DIM)
        mv = mkv[:, X_W:].reshape(B, n_mem, X_HEADS, HEAD_DIM)
        mkt = jnp.transpose(mk, (0, 2, 3, 1)).astype(BF16)
        mvh = jnp.transpose(mv, (0, 2, 1, 3)).astype(BF16)
        return mk, mv, mkt, mvh

    def cross_s(xq, i):
        return cross_sample(xq, jnp.transpose(cache_mem_k[i], (0, 2, 3, 1)), jnp.transpose(cache_mem_v[i], (0, 2, 3, 1)))

    mk0, mv0, mkt, mvh = mem_kv(0)
    qt, kh, vt, k_p, v_p, xq_p = sb_inproj(hp, g_mix[0], w_in_sb[0].astype(BF16))
    tok_p = sb_prompt_attention(b_sb[0], qt, kh, vt, B)
    xa_p = cross_prompt(xq_p, mkt, mvh, B)
    hp = outproj(tok_p, xa_p, hp, w_out[0].astype(BF16))

    zs = norm_matmul(hs, g_mix[0], w_in_sb[0])
    k_s, v_s, xq_s = zs[:, TOK_W:2 * TOK_W], zs[:, 2 * TOK_W:3 * TOK_W], zs[:, 3 * TOK_W:]
    tok_s = sb_sample_attention(zs[:, :TOK_W], b_sb[0],
                                jnp.transpose(cache_sb_k[0], (0, 2, 3, 1)),
                                jnp.transpose(cache_sb_v[0], (0, 2, 3, 1)), page_table)
    hs = outproj(tok_s, cross_s(xq_s, 0), hs, w_out[0])

    hp = ffn(hp, g_ffn[0], w_ffn_gu[0].astype(BF16), w_ffn_down[0].astype(BF16))
    hs = ffn(hs, g_ffn[0], w_ffn_gu[0], w_ffn_down[0])

    w = w_in_gla[0]
    lr_off = GLA_R_OFF + GLA_DV
    w_in = jnp.concatenate([w[:, :lr_off], w[:, lr_off + GLA_GATE_RANK:], w[:, lr_off:lr_off + GLA_GATE_RANK]],
                           axis=1)
    mk1, mv1, mkt, mvh = mem_kv(1)
    zp = norm_matmul(hp, g_mix[1], w_in.astype(BF16))
    tok_p, st_p = gla_prompt(zp.reshape(B, T, -1), w_gate_up[0], b_gate[0], g_gla_onorm[0])
    xa_p = cross_prompt(zp[:, GLA_XQ_OFF2:GLA_XQ_OFF2 + X_W], mkt, mvh, B)
    hp = outproj(tok_p.reshape(B * T, GLA_DV), xa_p, hp, w_out[1].astype(BF16))

    zs = norm_matmul(hs, g_mix[1], w_in)
    tok_s, st_s = gla_sample(zs, w_gate_up[0], b_gate[0], g_gla_onorm[0],
                             state_gla[0].reshape(Bs, GLA_DK, GLA_HDV))
    hs = outproj(tok_s, cross_s(zs[:, GLA_XQ_OFF2:GLA_XQ_OFF2 + X_W], 1), hs, w_out[1])

    w_gu = w_exp_gu[0].astype(BF16)
    w_dn = w_exp_down[0].astype(BF16)
    y_p = moe_final(hp, g_ffn[1], w_router[0], w_gu, w_dn, g_final, tm=512)
    y_s = moe_final(hs, g_ffn[1], w_router[0], w_gu, w_dn, g_final, tm=128)

    st_p = jnp.stack([jnp.stack([st_p[b, h * GLA_HDV:(h + 1) * GLA_HDV, h * GLA_HDK:(h + 1) * GLA_HDK].T
                                 for h in range(GLA_HEADS)]) for b in range(B)])
    return (y_p.reshape(B, T, D), y_s.reshape(Bs, 1, D),
            k_p.reshape(1, B, T, SB_HEADS, HEAD_DIM), v_p.reshape(1, B, T, SB_HEADS, HEAD_DIM),
            k_s.reshape(1, Bs, 1, SB_HEADS, HEAD_DIM), v_s.reshape(1, Bs, 1, SB_HEADS, HEAD_DIM),
            st_p[None], st_s.reshape(1, Bs, GLA_HEADS, GLA_HDK, GLA_HDV),
            jnp.stack([mk0, mk1]), jnp.stack([mv0, mv1]))
```

```python
import functools

import jax
import jax.numpy as jnp
from jax import lax
from jax.experimental import pallas as pl
from jax.experimental.pallas import tpu as pltpu

F32 = jnp.float32
BF16 = jnp.bfloat16

D_MODEL = 1024
HEAD_DIM = 64
X_HEADS = 4
X_W = X_HEADS * HEAD_DIM
TOK_W = D_MODEL - X_W
SB_HEADS = TOK_W // HEAD_DIM
GLA_HEADS = 4
GLA_DV = TOK_W
GLA_DK = GLA_DV // 2
GLA_HDK = GLA_DK // GLA_HEADS
GLA_HDV = GLA_DV // GLA_HEADS
GLA_GATE_RANK = 16
GLA_TAU = 16.0
GLA_CHUNK = 64
N_EXPERTS = 8
RMS_EPS = 1e-6
PAGE_SIZE = 128

LANES = 128
SB_KB = 256
VMEM_LIMIT = 56 * 1024 * 1024


def _cparams(sem):
    return pltpu.CompilerParams(dimension_semantics=sem, vmem_limit_bytes=VMEM_LIMIT)


def _rms(x, g):
    return x * lax.rsqrt(jnp.mean(x * x, axis=-1, keepdims=True) + RMS_EPS) * g


def _dot(a, b):
    return jnp.dot(a, b, preferred_element_type=F32)


def _split_bf16(x):
    hi = x.astype(BF16)
    lo = (x - hi.astype(F32)).astype(BF16)
    return hi, lo


def _dot_split(a, b):
    ah, al = _split_bf16(a)
    bh, bl = _split_bf16(b)
    return _dot(ah, bh) + (_dot(al, bh) + _dot(ah, bl))


def _matmul(a, w):
    if w.dtype == BF16:
        return _dot(a.astype(BF16), w)
    return _dot_split(a, w)


def _dot_nt(a, b):
    return lax.dot_general(a, b, (((1,), (1,)), ((), ())), preferred_element_type=F32)


def _dot_tn(a, b):
    return lax.dot_general(a, b, (((0,), (0,)), ((), ())), preferred_element_type=F32)


def _softplus(z):
    return jnp.maximum(z, 0.0) + jnp.log(1.0 + jnp.exp(-jnp.abs(z)))


def _tile(n, pref):
    t = min(n, pref)
    assert n % t == 0, (n, t)
    return t


def _norm_matmul_kernel(x_ref, g_ref, w_ref, o_ref):
    o_ref[...] = _matmul(_rms(x_ref[...], g_ref[...]), w_ref[...])


def norm_matmul(x, g, w_bf16, tm=512):
    n, d = x.shape
    m = w_bf16.shape[1]
    tm = _tile(n, tm)
    return pl.pallas_call(
        _norm_matmul_kernel,
        grid=(n // tm,),
        in_specs=[pl.BlockSpec((tm, d), lambda i: (i, 0)),
                  pl.BlockSpec((1, d), lambda i: (0, 0)),
                  pl.BlockSpec((d, m), lambda i: (0, 0))],
        out_specs=pl.BlockSpec((tm, m), lambda i: (i, 0)),
        out_shape=jax.ShapeDtypeStruct((n, m), F32),
        compiler_params=_cparams(("parallel",)),
        name="norm_matmul",
    )(x, g.reshape(1, d), w_bf16)


def _sb_inproj_kernel(x_ref, g_ref, w_ref, qt_ref, kh_ref, vt_ref, k_ref, v_ref, xq_ref):
    tm = x_ref.shape[0]
    xn = _rms(x_ref[...], g_ref[...]).astype(BF16)
    z = _dot(xn, w_ref[...])
    k_ref[...] = z[:, TOK_W:2 * TOK_W]
    v_ref[...] = z[:, 2 * TOK_W:3 * TOK_W]
    xq_ref[...] = z[:, 3 * TOK_W:]
    for h in range(SB_HEADS):
        lo = TOK_W + h * HEAD_DIM
        kh_ref[h] = z[:, lo:lo + HEAD_DIM].astype(BF16)
    for p in range(SB_HEADS // 2):
        qt2 = (z[:, p * LANES:(p + 1) * LANES] * (HEAD_DIM ** -0.5)).T
        vt2 = z[:, 2 * TOK_W + p * LANES:2 * TOK_W + (p + 1) * LANES].T
        for hh in range(2):
            rows = slice(hh * HEAD_DIM, (hh + 1) * HEAD_DIM)
            for c in range(tm // SB_KB):
                cols = slice(c * SB_KB, (c + 1) * SB_KB)
                qt_ref[2 * p + hh, c] = qt2[rows, cols].astype(BF16)
                vt_ref[2 * p + hh, c] = vt2[rows, cols].astype(BF16)


def sb_inproj(x, g, w_bf16, tm=512):
    n, d = x.shape
    m = w_bf16.shape[1]
    tm = _tile(n, tm)
    nkb = tm // SB_KB
    t_spec = pl.BlockSpec((SB_HEADS, nkb, HEAD_DIM, SB_KB), lambda i: (0, i, 0, 0))
    t_shape = jax.ShapeDtypeStruct((SB_HEADS, n // SB_KB, HEAD_DIM, SB_KB), BF16)
    return pl.pallas_call(
        _sb_inproj_kernel,
        grid=(n // tm,),
        in_specs=[pl.BlockSpec((tm, d), lambda i: (i, 0)),
                  pl.BlockSpec((1, d), lambda i: (0, 0)),
                  pl.BlockSpec((d, m), lambda i: (0, 0))],
        out_specs=[t_spec,
                   pl.BlockSpec((SB_HEADS, tm, HEAD_DIM), lambda i: (0, i, 0)),
                   t_spec,
                   pl.BlockSpec((tm, TOK_W), lambda i: (i, 0)),
                   pl.BlockSpec((tm, TOK_W), lambda i: (i, 0)),
                   pl.BlockSpec((tm, X_W), lambda i: (i, 0))],
        out_shape=[t_shape,
                   jax.ShapeDtypeStruct((SB_HEADS, n, HEAD_DIM), BF16),
                   t_shape,
                   jax.ShapeDtypeStruct((n, TOK_W), F32),
                   jax.ShapeDtypeStruct((n, TOK_W), F32),
                   jax.ShapeDtypeStruct((n, X_W), F32)],
        compiler_params=_cparams(("parallel",)),
        name="sb_inproj",
    )(x, g.reshape(1, d), w_bf16)


LOG2E = 1.4426950408889634


def _sb_prompt_kernel(bias_ref, qt_ref, k_ref, vt_ref, o_ref, z_sc, t_sc, sp_sc, a_sc):
    hp = pl.program_id(1)
    i = pl.program_id(2)
    tq = qt_ref.shape[-1]
    row = lax.broadcasted_iota(jnp.int32, (SB_KB, SB_KB), 0)
    col = lax.broadcasted_iota(jnp.int32, (SB_KB, SB_KB), 1)
    tri = (col > row).astype(BF16)
    causal = row < col
    heads = range(2)

    def scores(j, hh):
        k = k_ref[hh, pl.ds(pl.multiple_of(j * SB_KB, SB_KB), SB_KB), :]
        return _dot(k, qt_ref[hh, 0]) + bias_ref[2 * hp + hh]

    def stage1(z, hh, slot, mask):
        e = jnp.exp2(jnp.abs(z) * (-LOG2E))
        sp = jnp.maximum(z, 0.0) + jnp.log(1.0 + e)
        t = z - sp
        if mask is not None:
            sp = jnp.where(mask, sp, 0.0)
            t = jnp.where(mask, t, -jnp.inf)
        t_sc[slot, hh] = t
        sp_sc[hh] = sp.astype(BF16)
        return sp[0:8, :].astype(BF16).astype(F32)

    def tick(j1, slot, state, with_stage1):
        heads8, ws, carries, accs = state
        j3 = jnp.minimum(j1 + 2, i)
        pvs = [_dot(vt_ref[hh, j3], a_sc[hh]) for hh in heads]
        locs = [_dot(tri, sp_sc[hh]) for hh in heads]
        heads8_next = heads8
        if with_stage1:
            zs_next = [scores(jnp.maximum(j1 - 1, 0), hh) for hh in heads]
            heads8_next = [stage1(z_sc[slot, hh], hh, 1 - slot, None) for hh in heads]
            for hh in heads:
                z_sc[1 - slot, hh] = zs_next[hh]
        ws_next, new_c, new_a = [], [], []
        for hh in heads:
            a_sc[hh] = jnp.exp(t_sc[slot, hh] - locs[hh]).astype(BF16)
            ws_next.append(jnp.exp(-carries[hh]))
            total = locs[hh][0:1, :] + heads8[hh][0:1, :]
            new_c.append(carries[hh] + jnp.broadcast_to(total, carries[hh].shape))
            new_a.append(accs[hh] + ws[hh][0:1, :] * pvs[hh])
        return heads8_next, ws_next, new_c, new_a

    heads8 = [stage1(scores(i, hh), hh, 0, causal) for hh in heads]
    for hh in heads:
        z_sc[0, hh] = scores(jnp.maximum(i - 1, 0), hh)
        a_sc[hh] = jnp.zeros((SB_KB, tq), BF16)
    zrow = [jnp.minimum(h8, 0.0) for h8 in heads8]
    state = (heads8, zrow, zrow, [jnp.zeros((HEAD_DIM, tq), F32) for _ in heads])
    state = lax.fori_loop(0, i, lambda jj, st: tick(i - 1 - jj, jj & 1, st, True), state)
    _, ws, _, accs = tick(-1, i & 1, state, False)
    accs = [accs[hh] + ws[hh][0:1, :] * _dot(vt_ref[hh, 0], a_sc[hh]) for hh in heads]
    o_ref[...] = jnp.concatenate(accs, axis=0).T


def sb_prompt_attention(bias, qt, kh, vt, batch):
    n = kh.shape[1]
    t = n // batch
    tq = SB_KB
    nq = t // tq
    return pl.pallas_call(
        _sb_prompt_kernel,
        grid=(batch, SB_HEADS // 2, nq),
        in_specs=[pl.BlockSpec(memory_space=pltpu.SMEM),
                  pl.BlockSpec((2, 1, HEAD_DIM, tq), lambda b, h, i: (h, b * nq + i, 0, 0)),
                  pl.BlockSpec((2, t, HEAD_DIM), lambda b, h, i: (h, b, 0)),
                  pl.BlockSpec((2, nq, HEAD_DIM, SB_KB), lambda b, h, i: (h, b, 0, 0))],
        out_specs=pl.BlockSpec((tq, 2 * HEAD_DIM), lambda b, h, i: (b * nq + i, h)),
        out_shape=jax.ShapeDtypeStruct((n, TOK_W), F32),
        scratch_shapes=[pltpu.VMEM((2, 2, SB_KB, tq), F32), pltpu.VMEM((2, 2, SB_KB, tq), F32),
                        pltpu.VMEM((2, SB_KB, tq), BF16), pltpu.VMEM((2, SB_KB, tq), BF16)],
        compiler_params=_cparams(("parallel", "parallel", "arbitrary")),
        name="sb_prompt_attention",
    )(bias, qt, kh, vt)


SB_PAGES_PER_STEP = 8


def _split3_bf16(x):
    hi = x.astype(BF16)
    r = x - hi.astype(F32)
    mid = r.astype(BF16)
    lo = (r - mid.astype(F32)).astype(BF16)
    return hi, mid, lo


def _sb_sample_kernel(pt_ref, q_ref, bias_ref, *refs):
    del pt_ref
    npg = (len(refs) - 3) // 2
    k_refs, v_refs = refs[:npg], refs[npg:2 * npg]
    o_ref, carry_sc, acc_sc = refs[2 * npg:]
    g = pl.program_id(1)
    hpad = carry_sc.shape[0]

    @pl.when(g == 0)
    def _():
        carry_sc[...] = jnp.zeros_like(carry_sc)
        acc_sc[...] = jnp.zeros_like(acc_sc)

    r = lax.broadcasted_iota(jnp.int32, (PAGE_SIZE, PAGE_SIZE), 0)
    c = lax.broadcasted_iota(jnp.int32, (PAGE_SIZE, PAGE_SIZE), 1)
    tri = (r > c).astype(BF16)
    qs = [_row_to_col(q_ref[0, h:h + 1, :]) * (HEAD_DIM ** -0.5) for h in range(SB_HEADS)]
    pad = jnp.zeros((hpad - SB_HEADS, PAGE_SIZE), F32)
    carry = carry_sc[...]
    for p in range(npg):
        rows = [jnp.sum(k_refs[p][0, h] * qs[h], axis=0, keepdims=True) for h in range(SB_HEADS)]
        z = jnp.concatenate(rows + [pad], axis=0) + bias_ref[...]
        sp = _softplus(z)
        hi, mid, lo = _split3_bf16(sp)
        loc = _dot(hi, tri) + (_dot(mid, tri) + _dot(lo, tri))
        a = jnp.exp(z - sp - loc - carry)
        for h in range(SB_HEADS):
            acc_sc[h] += a[h:h + 1, :] * v_refs[p][0, h]
        carry = carry + jnp.broadcast_to(loc[:, 0:1] + sp[:, 0:1], carry.shape)
    carry_sc[...] = carry

    @pl.when(g == pl.num_programs(1) - 1)
    def _():
        o_ref[0] = jnp.concatenate(
            [_col_to_row(jnp.sum(acc_sc[h], axis=-1, keepdims=True)) for h in range(SB_HEADS)], axis=0)


def sb_sample_attention(q, bias, cache_k, cache_v, page_table):
    bs = q.shape[0]
    n_pages = page_table.shape[1]
    npg = _tile(n_pages, SB_PAGES_PER_STEP)
    hpad = 16
    bias_b = jnp.zeros((hpad, PAGE_SIZE), F32).at[:SB_HEADS].set(
        jnp.broadcast_to(bias[:, None], (SB_HEADS, PAGE_SIZE)))
    pt = page_table.reshape(-1)

    def page_spec(p):
        def index_map(b, g, pt_ref):
            return (pt_ref[b * n_pages + (n_pages - 1 - g * npg - p)], 0, 0, 0)
        return pl.BlockSpec((1, SB_HEADS, HEAD_DIM, PAGE_SIZE), index_map)

    qo_spec = pl.BlockSpec((1, SB_HEADS, HEAD_DIM), lambda b, g, pt_ref: (b, 0, 0))
    grid_spec = pltpu.PrefetchScalarGridSpec(
        num_scalar_prefetch=1,
        grid=(bs, n_pages // npg),
        in_specs=[qo_spec, pl.BlockSpec((hpad, PAGE_SIZE), lambda b, g, pt_ref: (0, 0))]
        + [page_spec(p) for p in range(npg)] * 2,
        out_specs=qo_spec,
        scratch_shapes=[pltpu.VMEM((hpad, PAGE_SIZE), F32), pltpu.VMEM((SB_HEADS, HEAD_DIM, PAGE_SIZE), F32)],
    )
    out = pl.pallas_call(
        _sb_sample_kernel,
        grid_spec=grid_spec,
        out_shape=jax.ShapeDtypeStruct((bs, SB_HEADS, HEAD_DIM), F32),
        compiler_params=_cparams(("parallel", "arbitrary")),
        name="sb_sample_attention",
    )(pt, q.reshape(bs, SB_HEADS, HEAD_DIM), bias_b, *([cache_k] * npg), *([cache_v] * npg))
    return out.reshape(bs, TOK_W)


def _softmax_rows(s):
    m = jnp.max(s, axis=-1, keepdims=True)
    e = jnp.exp(s - m)
    return e / jnp.sum(e, axis=-1, keepdims=True)


def _cross_prompt_kernel(q_ref, mkt_ref, mv_ref, o_ref):
    q = q_ref[...]
    for h in range(X_HEADS):
        lo = h * HEAD_DIM
        qh = q[:, lo:lo + HEAD_DIM].astype(BF16)
        s = _dot(qh, mkt_ref[0, h]) * (HEAD_DIM ** -0.5)
        p = _softmax_rows(s).astype(BF16)
        o_ref[:, lo:lo + HEAD_DIM] = _dot(p, mv_ref[0, h])


def cross_prompt(xq, mkt, mvh, batch, tq=512):
    n = xq.shape[0]
    t = n // batch
    tq = _tile(t, tq)
    nq = t // tq
    nm = mkt.shape[-1]
    return pl.pallas_call(
        _cross_prompt_kernel,
        grid=(batch, nq),
        in_specs=[pl.BlockSpec((tq, X_W), lambda b, i: (b * nq + i, 0)),
                  pl.BlockSpec((1, X_HEADS, HEAD_DIM, nm), lambda b, i: (b, 0, 0, 0)),
                  pl.BlockSpec((1, X_HEADS, nm, HEAD_DIM), lambda b, i: (b, 0, 0, 0))],
        out_specs=pl.BlockSpec((tq, X_W), lambda b, i: (b * nq + i, 0)),
        out_shape=jax.ShapeDtypeStruct((n, X_W), F32),
        compiler_params=_cparams(("parallel", "parallel")),
        name="cross_prompt",
    )(xq, mkt, mvh)


def _eye(n):
    return lax.broadcasted_iota(jnp.int32, (n, n), 0) == lax.broadcasted_iota(jnp.int32, (n, n), 1)


def _row_to_col(row):
    n = row.shape[1]
    return jnp.sum(jnp.where(_eye(n), jnp.broadcast_to(row, (n, n)), 0.0), axis=1, keepdims=True)


def _col_to_row(col):
    n = col.shape[0]
    return jnp.sum(jnp.where(_eye(n), jnp.broadcast_to(col, (n, n)), 0.0), axis=0, keepdims=True)


def _cross_sample_kernel(q_ref, mk_ref, mv_ref, o_ref):
    for j in range(q_ref.shape[0]):
        rows = []
        for h in range(X_HEADS):
            q = _row_to_col(q_ref[j, h:h + 1, :])
            s = jnp.sum(mk_ref[0, j, h] * q, axis=0, keepdims=True) * (HEAD_DIM ** -0.5)
            p = _softmax_rows(s)
            rows.append(_col_to_row(jnp.sum(mv_ref[0, j, h] * p, axis=-1, keepdims=True)))
        o_ref[j] = jnp.concatenate(rows, axis=0)


def cross_sample(xq, mk, mv, layer, g=8):
    bs = xq.shape[0]
    g = _tile(bs, g)
    nm = mk.shape[-1]
    qo_spec = pl.BlockSpec((g, X_HEADS, HEAD_DIM), lambda i: (i, 0, 0))
    m_spec = pl.BlockSpec((1, g, X_HEADS, HEAD_DIM, nm), lambda i: (layer, i, 0, 0, 0))
    out = pl.pallas_call(
        _cross_sample_kernel,
        grid=(bs // g,),
        in_specs=[qo_spec, m_spec, m_spec],
        out_specs=qo_spec,
        out_shape=jax.ShapeDtypeStruct((bs, X_HEADS, HEAD_DIM), F32),
        compiler_params=_cparams(("parallel",)),
        name="cross_sample",
    )(xq.reshape(bs, X_HEADS, HEAD_DIM), mk, mv)
    return out.reshape(bs, X_W)


def _outproj_kernel(tok_ref, xa_ref, h_ref, wt_ref, wb_ref, o_ref):
    o_ref[...] = h_ref[...] + _matmul(tok_ref[...], wt_ref[...]) + _matmul(xa_ref[...], wb_ref[...])


def outproj(tok, xa, h, w_bf16, tm=512):
    n, d = h.shape
    tm = _tile(n, tm)
    wt, wb = w_bf16[:TOK_W], w_bf16[TOK_W:]
    return pl.pallas_call(
        _outproj_kernel,
        grid=(n // tm,),
        in_specs=[pl.BlockSpec((tm, TOK_W), lambda i: (i, 0)),
                  pl.BlockSpec((tm, X_W), lambda i: (i, 0)),
                  pl.BlockSpec((tm, d), lambda i: (i, 0)),
                  pl.BlockSpec((TOK_W, d), lambda i: (0, 0)),
                  pl.BlockSpec((X_W, d), lambda i: (0, 0))],
        out_specs=pl.BlockSpec((tm, d), lambda i: (i, 0)),
        out_shape=jax.ShapeDtypeStruct((n, d), F32),
        compiler_params=_cparams(("parallel",)),
        name="outproj",
    )(tok, xa, h, wt, wb)


def _ffn_kernel(x_ref, g_ref, wg_ref, wu_ref, wd_ref, o_ref, xn_sc, acc_sc):
    f = pl.program_id(1)

    @pl.when(f == 0)
    def _():
        xn_sc[...] = _rms(x_ref[...], g_ref[...]).astype(xn_sc.dtype)
        acc_sc[...] = jnp.zeros_like(acc_sc)

    xn = xn_sc[...]
    gate = _matmul(xn, wg_ref[...])
    up = _matmul(xn, wu_ref[...])
    acc_sc[...] += _matmul(jax.nn.silu(gate) * up, wd_ref[...])

    @pl.when(f == pl.num_programs(1) - 1)
    def _():
        o_ref[...] = x_ref[...] + acc_sc[...]


def ffn(x, g, w_gu, w_down, tm=1024, tf=256):
    n, d = x.shape
    dff = w_down.shape[0]
    tm = _tile(n, tm)
    nf = dff // tf
    return pl.pallas_call(
        _ffn_kernel,
        grid=(n // tm, nf),
        in_specs=[pl.BlockSpec((tm, d), lambda i, f: (i, 0)),
                  pl.BlockSpec((1, d), lambda i, f: (0, 0)),
                  pl.BlockSpec((d, tf), lambda i, f: (0, f)),
                  pl.BlockSpec((d, tf), lambda i, f: (0, f + nf)),
                  pl.BlockSpec((tf, d), lambda i, f: (f, 0))],
        out_specs=pl.BlockSpec((tm, d), lambda i, f: (i, 0)),
        out_shape=jax.ShapeDtypeStruct((n, d), F32),
        scratch_shapes=[pltpu.VMEM((tm, d), w_gu.dtype), pltpu.VMEM((tm, d), F32)],
        compiler_params=_cparams(("parallel", "arbitrary")),
        name="ffn",
    )(x, g.reshape(1, d), w_gu, w_gu, w_down)


def _router_kernel(x_ref, g_ref, w_ref, o_ref, cnt_ref, cnt_sc):
    @pl.when(pl.program_id(0) == 0)
    def _():
        cnt_sc[...] = jnp.zeros_like(cnt_sc)

    xn = _rms(x_ref[...], g_ref[...])
    logits = _dot_split(xn, w_ref[...])
    lane = lax.broadcasted_iota(jnp.int32, logits.shape, 1)
    neg = jnp.float32(-jnp.inf)
    logits = jnp.where(lane < N_EXPERTS, logits, neg)
    m1 = jnp.max(logits, axis=-1, keepdims=True)
    i1 = jnp.min(jnp.where(logits == m1, lane, LANES), axis=-1, keepdims=True)
    rest = jnp.where(lane == i1, neg, logits)
    m2 = jnp.max(rest, axis=-1, keepdims=True)
    i2 = jnp.min(jnp.where(rest == m2, lane, LANES), axis=-1, keepdims=True)
    e2 = jnp.exp(m2 - m1)
    den = 1.0 + e2
    oh1 = lane == i1
    oh2 = lane == i2
    cnt = (oh1 | oh2).astype(BF16)
    tm = cnt.shape[0]
    r = lax.broadcasted_iota(jnp.int32, (tm, tm), 0)
    c = lax.broadcasted_iota(jnp.int32, (tm, tm), 1)
    before = _dot((c < r).astype(BF16), cnt) + cnt_sc[...]
    r1 = jnp.sum(jnp.where(oh1, before, 0.0), axis=-1, keepdims=True)
    r2 = jnp.sum(jnp.where(oh2, before, 0.0), axis=-1, keepdims=True)
    cnt_sc[...] = before[tm - 1:tm, :] + cnt[tm - 1:tm, :].astype(F32)
    cnt_ref[...] = cnt_sc[...]
    info = jnp.zeros(logits.shape, F32)
    for k, val in enumerate((i1.astype(F32), i2.astype(F32), 1.0 / den, e2 / den, r1, r2)):
        info = jnp.where(lane == k, val, info)
    o_ref[...] = info


R_E1, R_E2, R_G1, R_G2, R_R1, R_R2 = range(6)


def router(x, g, w_router, tm=512):
    n, d = x.shape
    tm = _tile(n, tm)
    w_pad = jnp.zeros((d, LANES), F32).at[:, :N_EXPERTS].set(w_router)
    return pl.pallas_call(
        _router_kernel,
        grid=(n // tm,),
        in_specs=[pl.BlockSpec((tm, d), lambda i: (i, 0)),
                  pl.BlockSpec((1, d), lambda i: (0, 0)),
                  pl.BlockSpec((d, LANES), lambda i: (0, 0))],
        out_specs=[pl.BlockSpec((tm, LANES), lambda i: (i, 0)),
                   pl.BlockSpec((1, LANES), lambda i: (0, 0))],
        out_shape=[jax.ShapeDtypeStruct((n, LANES), F32), jax.ShapeDtypeStruct((1, LANES), F32)],
        scratch_shapes=[pltpu.VMEM((1, LANES), F32)],
        compiler_params=_cparams(("arbitrary",)),
        name="router",
    )(x, g.reshape(1, d), w_pad)


def _row_copy(src_hbm, src_row, dst_ref, dst_row, sem):
    return pltpu.make_async_copy(src_hbm.at[pl.ds(src_row, 1)], dst_ref.at[pl.ds(dst_row, 1)], sem)


def _expert_kernel(te_ref, nv_ref, src_ref, x_hbm, g_ref, wg_ref, wu_ref, wd_ref, o_ref, xbuf, xn_sc, acc_sc, sem):
    del te_ref
    t = pl.program_id(0)
    f = pl.program_id(1)
    nf = pl.num_programs(1)
    tm = xbuf.shape[1]
    n_valid = nv_ref[0]
    half = t % 2

    def fetch(tile, first_row, n_rows, dst_half):
        for j in range(n_rows):
            r = first_row + j
            _row_copy(x_hbm, src_ref[tile * tm + r], xbuf.at[dst_half], r, sem.at[dst_half]).start()

    @pl.when((t == 0) & (f == 0))
    def _():
        fetch(0, 0, tm, 0)

    @pl.when((f == 0) & (t <= n_valid))
    def _():
        def wait(r, carry):
            _row_copy(x_hbm, 0, xbuf.at[half], r, sem.at[half]).wait()
            return carry
        lax.fori_loop(0, tm, wait, 0, unroll=8)

    @pl.when(t < n_valid)
    def _():
        part = tm // nf
        fetch(t + 1, f * part, part, 1 - half)

        @pl.when(f == 0)
        def _():
            xn_sc[...] = _rms(xbuf[half], g_ref[...]).astype(BF16)
            acc_sc[...] = jnp.zeros_like(acc_sc)

        xn = xn_sc[...]
        gate = _dot(xn, wg_ref[0])
        up = _dot(xn, wu_ref[0])
        act = (jax.nn.silu(gate) * up).astype(BF16)
        acc_sc[...] += _dot(act, wd_ref[0])

        @pl.when(f == nf - 1)
        def _():
            o_ref[...] = acc_sc[...]

    @pl.when((t >= n_valid) & (f == nf - 1))
    def _():
        o_ref[...] = jnp.zeros_like(o_ref)


def expert_swiglu(x, g, src, tile_expert, n_valid, w_gu_bf16, w_down_bf16, tm, tf=896):
    n_slots = src.shape[0]
    d = x.shape[1]
    dff = w_down_bf16.shape[1]
    nf = dff // tf
    assert tm % nf == 0

    def wmap(off):
        def index_map(t, f, te_ref, nv_ref, src_ref):
            return (te_ref[t], 0, jnp.where(t < nv_ref[0], f, nf - 1) + off)
        return index_map

    def dmap(t, f, te_ref, nv_ref, src_ref):
        return (te_ref[t], jnp.where(t < nv_ref[0], f, nf - 1), 0)

    grid_spec = pltpu.PrefetchScalarGridSpec(
        num_scalar_prefetch=3,
        grid=(n_slots // tm, nf),
        in_specs=[pl.BlockSpec(memory_space=pl.ANY),
                  pl.BlockSpec((1, d), lambda t, f, te_ref, nv_ref, src_ref: (0, 0)),
                  pl.BlockSpec((1, d, tf), wmap(0)),
                  pl.BlockSpec((1, d, tf), wmap(nf)),
                  pl.BlockSpec((1, tf, d), dmap)],
        out_specs=pl.BlockSpec((tm, d), lambda t, f, te_ref, nv_ref, src_ref: (t, 0)),
        scratch_shapes=[pltpu.VMEM((2, tm, d), F32), pltpu.VMEM((tm, d), BF16), pltpu.VMEM((tm, d), F32),
                        pltpu.SemaphoreType.DMA((2,))],
    )
    return pl.pallas_call(
        _expert_kernel,
        grid_spec=grid_spec,
        out_shape=jax.ShapeDtypeStruct((n_slots, d), F32),
        compiler_params=_cparams(("arbitrary", "arbitrary")),
        name="expert_swiglu",
    )(tile_expert, n_valid, src, x, g.reshape(1, d), w_gu_bf16, w_gu_bf16, w_down_bf16)


def _combine_kernel(s1_ref, s2_ref, x_ref, info_ref, gf_ref, ys_hbm, o_ref, buf, sem):
    tc = x_ref.shape[0]
    base = pl.program_id(0) * tc

    def start(r, carry):
        _row_copy(ys_hbm, s1_ref[base + r], buf.at[0], r, sem).start()
        _row_copy(ys_hbm, s2_ref[base + r], buf.at[1], r, sem).start()
        return carry

    def wait(r, carry):
        _row_copy(ys_hbm, 0, buf.at[0], r, sem).wait()
        _row_copy(ys_hbm, 0, buf.at[1], r, sem).wait()
        return carry

    lax.fori_loop(0, tc, start, 0, unroll=8)
    lax.fori_loop(0, tc, wait, 0, unroll=8)
    info = info_ref[...]
    g1 = info[:, R_G1:R_G1 + 1]
    g2 = info[:, R_G2:R_G2 + 1]
    o_ref[...] = _rms(x_ref[...] + (g1 * buf[0] + g2 * buf[1]), gf_ref[...])


def combine_final(x, info, slot1, slot2, ys, g_final, tc=256):
    n, d = x.shape
    tc = _tile(n, tc)
    grid_spec = pltpu.PrefetchScalarGridSpec(
        num_scalar_prefetch=2,
        grid=(n // tc,),
        in_specs=[pl.BlockSpec((tc, d), lambda i, s1, s2: (i, 0)),
                  pl.BlockSpec((tc, LANES), lambda i, s1, s2: (i, 0)),
                  pl.BlockSpec((1, d), lambda i, s1, s2: (0, 0)),
                  pl.BlockSpec(memory_space=pl.ANY)],
        out_specs=pl.BlockSpec((tc, d), lambda i, s1, s2: (i, 0)),
        scratch_shapes=[pltpu.VMEM((2, tc, d), F32), pltpu.SemaphoreType.DMA(())],
    )
    return pl.pallas_call(
        _combine_kernel,
        grid_spec=grid_spec,
        out_shape=jax.ShapeDtypeStruct((n, d), F32),
        compiler_params=_cparams(("arbitrary",)),
        name="combine_final",
    )(slot1, slot2, x, info, g_final.reshape(1, d), ys)


def moe_final(x, g, w_router, w_gu_bf16, w_down_bf16, g_final, tm):
    n, d = x.shape
    info, counts = router(x, g, w_router)
    counts = counts[0, :N_EXPERTS].astype(jnp.int32)
    padded = ((counts + tm - 1) // tm) * tm
    ends = jnp.cumsum(padded)
    offs = ends - padded
    n_slots = (-(-2 * n // tm) + N_EXPERTS) * tm
    e1 = info[:, R_E1].astype(jnp.int32)
    e2 = info[:, R_E2].astype(jnp.int32)
    slot1 = offs[e1] + info[:, R_R1].astype(jnp.int32)
    slot2 = offs[e2] + info[:, R_R2].astype(jnp.int32)
    tok = jnp.arange(n, dtype=jnp.int32)
    src = jnp.zeros((n_slots,), jnp.int32).at[jnp.concatenate([slot1, slot2])].set(
        jnp.concatenate([tok, tok]))
    tile_start = jnp.arange(n_slots // tm, dtype=jnp.int32) * tm
    n_valid = (ends[-1] // tm).reshape(1)
    tile_expert = jnp.minimum(jnp.sum(tile_start[:, None] >= ends[None, :], axis=1), N_EXPERTS - 1).astype(jnp.int32)
    tile_expert = jnp.where(tile_start < ends[-1], tile_expert, tile_expert[jnp.maximum(n_valid[0] - 1, 0)])
    ys = expert_swiglu(x, g, src, tile_expert, n_valid, w_gu_bf16, w_down_bf16, tm)
    return combine_final(x, info, slot1, slot2, ys, g_final)


GLA_R_OFF = 2 * GLA_DK + GLA_DV
GLA_XQ_OFF2 = GLA_R_OFF + GLA_DV
GLA_LR_OFF2 = GLA_XQ_OFF2 + X_W


def _head_of(col, width):
    return ((col >= width).astype(jnp.int32) + (col >= 2 * width).astype(jnp.int32)
            + (col >= 3 * width).astype(jnp.int32))


def _gla_log_decay(lr, wgu, bg):
    gl = _matmul(lr, wgu) + bg
    return -_softplus(-gl) / GLA_TAU


def _gla_out(o, r, gon):
    col = lax.broadcasted_iota(jnp.int32, (1, GLA_DV), 1)
    hv = _head_of(col, GLA_HDV)
    o2 = o * o
    inv = jnp.zeros_like(o)
    for h in range(GLA_HEADS):
        m = hv == h
        ms = jnp.sum(jnp.where(m, o2, 0.0), axis=-1, keepdims=True) * (1.0 / GLA_HDV)
        inv = jnp.where(m, lax.rsqrt(ms + RMS_EPS), inv)
    return (o * inv * gon) * jax.nn.silu(r)


def _gla_prompt_kernel(z_ref, wgu_ref, bg_ref, gon_ref, bd_ref, tok_ref, st_ref, st_sc):
    nb = z_ref.shape[0]
    gt = z_ref.shape[1]
    c = GLA_CHUNK

    @pl.when(pl.program_id(0) == 0)
    def _():
        st_sc[...] = jnp.zeros_like(st_sc)

    rr = lax.broadcasted_iota(jnp.int32, (c, c), 0)
    cc = lax.broadcasted_iota(jnp.int32, (c, c), 1)
    ltri = (cc <= rr).astype(BF16)
    r4 = lax.broadcasted_iota(jnp.int32, (c, GLA_HEADS * c), 0)
    c4 = lax.broadcasted_iota(jnp.int32, (c, GLA_HEADS * c), 1)
    intra_mask = (c4 & (c - 1)) <= r4
    hk = _head_of(lax.broadcasted_iota(jnp.int32, (1, GLA_DK), 1), GLA_HDK)
    hv = _head_of(lax.broadcasted_iota(jnp.int32, (1, GLA_DV), 1), GLA_HDV)
    wgu = wgu_ref[...]
    bg = bg_ref[...]
    gon = gon_ref[...]
    bd = bd_ref[...]

    for ci in range(gt // c):
        for b in range(nb):
            zc = z_ref[b, ci * c:(ci + 1) * c, :]
            q = zc[:, 0:GLA_DK] * (GLA_HDK ** -0.5)
            k = zc[:, GLA_DK:2 * GLA_DK]
            v = zc[:, 2 * GLA_DK:GLA_R_OFF]
            r = zc[:, GLA_R_OFF:GLA_XQ_OFF2]
            lr = zc[:, GLA_LR_OFF2:GLA_LR_OFF2 + GLA_GATE_RANK]
            la = _gla_log_decay(lr, wgu, bg)
            hi = la.astype(BF16)
            r1 = la - hi.astype(F32)
            mid = r1.astype(BF16)
            low = (r1 - mid.astype(F32)).astype(BF16)
            bc = _dot(ltri, hi) + (_dot(ltri, mid) + _dot(ltri, low))
            qe = (q * jnp.exp(bc)).astype(BF16)
            ke = (k * jnp.exp(-bc)).astype(BF16)
            vb = v.astype(BF16)
            ke_stack = jnp.concatenate(
                [jnp.where(hk == h, ke, jnp.zeros_like(ke)) for h in range(GLA_HEADS)], axis=0)
            v_stack = jnp.concatenate(
                [jnp.where(hv == h, vb, jnp.zeros_like(vb)) for h in range(GLA_HEADS)], axis=0)
            scores = jnp.where(intra_mask, _dot_nt(qe, ke_stack), 0.0)
            o_intra = _dot(scores.astype(BF16), v_stack)
            st = st_sc[b]
            o_inter = _dot_nt(qe, st.astype(BF16))
            tok_ref[b, ci * c:(ci + 1) * c, :] = _gla_out(o_inter + o_intra, r, gon)
            b_last = bc[c - 1:c, :]
            kd = (k * jnp.exp(b_last - bc)).astype(BF16)
            st_sc[b] = st * jnp.exp(b_last) + _dot_tn(vb, kd) * bd

    @pl.when(pl.program_id(0) == pl.num_programs(0) - 1)
    def _():
        st_ref[...] = st_sc[...]


def gla_prompt(z, w_gate_up, b_gate, g_onorm, gt=256):
    nb, t, zw = z.shape
    gt = _tile(t, gt)
    rv = lax.broadcasted_iota(jnp.int32, (GLA_DV, GLA_DK), 0) // GLA_HDV
    ck = lax.broadcasted_iota(jnp.int32, (GLA_DV, GLA_DK), 1) // GLA_HDK
    bd = (rv == ck).astype(F32)
    return pl.pallas_call(
        _gla_prompt_kernel,
        grid=(t // gt,),
        in_specs=[pl.BlockSpec((nb, gt, zw), lambda i: (0, i, 0)),
                  pl.BlockSpec((GLA_GATE_RANK, GLA_DK), lambda i: (0, 0)),
                  pl.BlockSpec((1, GLA_DK), lambda i: (0, 0)),
                  pl.BlockSpec((1, GLA_DV), lambda i: (0, 0)),
                  pl.BlockSpec((GLA_DV, GLA_DK), lambda i: (0, 0))],
        out_specs=[pl.BlockSpec((nb, gt, GLA_DV), lambda i: (0, i, 0)),
                   pl.BlockSpec((nb, GLA_DV, GLA_DK), lambda i: (0, 0, 0))],
        out_shape=[jax.ShapeDtypeStruct((nb, t, GLA_DV), F32),
                   jax.ShapeDtypeStruct((nb, GLA_DV, GLA_DK), F32)],
        scratch_shapes=[pltpu.VMEM((nb, GLA_DV, GLA_DK), F32)],
        compiler_params=_cparams(("arbitrary",)),
        name="gla_prompt",
    )(z, w_gate_up.astype(BF16), b_gate.reshape(1, GLA_DK), jnp.tile(g_onorm, GLA_HEADS).reshape(1, GLA_DV), bd)


def _gla_gate_kernel(z_ref, wgu_ref, bg_ref, q_ref, ea_ref):
    z = z_ref[...]
    la = _gla_log_decay(z[:, GLA_LR_OFF2:GLA_LR_OFF2 + GLA_GATE_RANK], wgu_ref[...], bg_ref[...])
    ea_ref[...] = jnp.exp(la)
    q_ref[...] = z[:, 0:GLA_DK] * (GLA_HDK ** -0.5)


def _gla_sample_kernel(q_ref, k_ref, ea_ref, z_ref, gon_ref, s0_ref, tok_ref, s_ref):
    g = q_ref.shape[0]
    for j in range(g):
        zr = z_ref[j]
        v = zr[:, 2 * GLA_DK:GLA_R_OFF]
        r = zr[:, GLA_R_OFF:GLA_XQ_OFF2]
        v_rows = jnp.concatenate(
            [jnp.broadcast_to(v[:, h * GLA_HDV:(h + 1) * GLA_HDV], (GLA_HDK, GLA_HDV))
             for h in range(GLA_HEADS)], axis=0)
        s_new = ea_ref[j] * s0_ref[j] + k_ref[j] * v_rows
        s_ref[j] = s_new
        qs = q_ref[j] * s_new
        o = jnp.concatenate(
            [jnp.sum(qs[h * GLA_HDK:(h + 1) * GLA_HDK], axis=0, keepdims=True) for h in range(GLA_HEADS)],
            axis=1)
        tok_ref[j] = _gla_out(o, r, gon_ref[...])


def gla_sample(z, w_gate_up, b_gate, g_onorm, state0, g=8):
    bs, zw = z.shape
    g = _tile(bs, g)
    q, ea = pl.pallas_call(
        _gla_gate_kernel,
        out_shape=[jax.ShapeDtypeStruct((bs, GLA_DK), F32), jax.ShapeDtypeStruct((bs, GLA_DK), F32)],
        name="gla_gate",
    )(z, w_gate_up, b_gate.reshape(1, GLA_DK))
    k = z[:, GLA_DK:2 * GLA_DK]
    col = lambda a: a.reshape(bs, GLA_DK, 1)
    col_spec = pl.BlockSpec((g, GLA_DK, 1), lambda i: (i, 0, 0))
    tok, s_new = pl.pallas_call(
        _gla_sample_kernel,
        grid=(bs // g,),
        in_specs=[col_spec, col_spec, col_spec,
                  pl.BlockSpec((g, 1, zw), lambda i: (i, 0, 0)),
                  pl.BlockSpec((1, GLA_DV), lambda i: (0, 0)),
                  pl.BlockSpec((g, GLA_DK, GLA_HDV), lambda i: (i, 0, 0))],
        out_specs=[pl.BlockSpec((g, 1, GLA_DV), lambda i: (i, 0, 0)),
                   pl.BlockSpec((g, GLA_DK, GLA_HDV), lambda i: (i, 0, 0))],
        out_shape=[jax.ShapeDtypeStruct((bs, 1, GLA_DV), F32),
                   jax.ShapeDtypeStruct((bs, GLA_DK, GLA_HDV), F32)],
        compiler_params=_cparams(("parallel",)),
        name="gla_sample",
    )(col(q), col(k), col(ea), z.reshape(bs, 1, zw), jnp.tile(g_onorm, GLA_HEADS).reshape(1, GLA_DV), state0)
    return tok.reshape(bs, GLA_DV), s_new


def kernel(x_prompt, x_sample, cache_sb_k, cache_sb_v, state_gla, cache_mem_k, cache_mem_v, page_table, mem_prompt,
           g_mix, g_mem, w_mem_kv, w_in_sb, b_sb, w_in_gla, w_gate_up, b_gate, g_gla_onorm, w_out,
           g_ffn, w_ffn_gu, w_ffn_down, w_router, w_exp_gu, w_exp_down, g_final):
    B, T, D = x_prompt.shape
    Bs = x_sample.shape[0]
    n_mem = mem_prompt.shape[1]
    n_phys = cache_sb_k.shape[1]
    hp = x_prompt.reshape(B * T, D)
    hs = x_sample.reshape(Bs, D)
    mem = mem_prompt.reshape(B * n_mem, D)

    def mem_kv(i):
        mkv = norm_matmul(mem, g_mem[i], w_mem_kv[i].astype(BF16))
        mk = mkv[:, :X_W].reshape(B, n_mem, X_HEADS, HEAD_DIM)
        mv = mkv[:, X_W:].reshape(B, n_mem, X_HEADS, HEAD_DIM)
        mkt = jnp.transpose(mk, (0, 2, 3, 1)).astype(BF16)
        mvh = jnp.transpose(mv, (0, 2, 1, 3)).astype(BF16)
        return mk, mv, mkt, mvh

    def cross_s(xq, i):
        return cross_sample(xq, mem_k_t, mem_v_t, i)

    mem_k_t = jnp.transpose(cache_mem_k, (0, 1, 3, 4, 2))
    mem_v_t = jnp.transpose(cache_mem_v, (0, 1, 3, 4, 2))

    mk0, mv0, mkt, mvh = mem_kv(0)
    qt, kh, vt, k_p, v_p, xq_p = sb_inproj(hp, g_mix[0], w_in_sb[0].astype(BF16))
    tok_p = sb_prompt_attention(b_sb[0], qt, kh, vt, B)
    xa_p = cross_prompt(xq_p, mkt, mvh, B)
    hp = outproj(tok_p, xa_p, hp, w_out[0].astype(BF16))

    zs = norm_matmul(hs, g_mix[0], w_in_sb[0])
    k_s, v_s, xq_s = zs[:, TOK_W:2 * TOK_W], zs[:, 2 * TOK_W:3 * TOK_W], zs[:, 3 * TOK_W:]
    tok_s = sb_sample_attention(zs[:, :TOK_W], b_sb[0],
                                jnp.transpose(cache_sb_k[0], (0, 2, 3, 1)),
                                jnp.transpose(cache_sb_v[0], (0, 2, 3, 1)), page_table)
    hs = outproj(tok_s, cross_s(xq_s, 0), hs, w_out[0])

    hp = ffn(hp, g_ffn[0], w_ffn_gu[0].astype(BF16), w_ffn_down[0].astype(BF16))
    hs = ffn(hs, g_ffn[0], w_ffn_gu[0], w_ffn_down[0])

    w = w_in_gla[0]
    lr_off = GLA_R_OFF + GLA_DV
    w_in = jnp.concatenate([w[:, :lr_off], w[:, lr_off + GLA_GATE_RANK:], w[:, lr_off:lr_off + GLA_GATE_RANK]],
                           axis=1)
    mk1, mv1, mkt, mvh = mem_kv(1)
    zp = norm_matmul(hp, g_mix[1], w_in.astype(BF16))
    tok_p, st_p = gla_prompt(zp.reshape(B, T, -1), w_gate_up[0], b_gate[0], g_gla_onorm[0])
    xa_p = cross_prompt(zp[:, GLA_XQ_OFF2:GLA_XQ_OFF2 + X_W], mkt, mvh, B)
    hp = outproj(tok_p.reshape(B * T, GLA_DV), xa_p, hp, w_out[1].astype(BF16))

    zs = norm_matmul(hs, g_mix[1], w_in)
    tok_s, st_s = gla_sample(zs, w_gate_up[0], b_gate[0], g_gla_onorm[0],
                             state_gla[0].reshape(Bs, GLA_DK, GLA_HDV))
    hs = outproj(tok_s, cross_s(zs[:, GLA_XQ_OFF2:GLA_XQ_OFF2 + X_W], 1), hs, w_out[1])

    w_gu = w_exp_gu[0].astype(BF16)
    w_dn = w_exp_down[0].astype(BF16)
    y_p = moe_final(hp, g_ffn[1], w_router[0], w_gu, w_dn, g_final, tm=512)
    y_s = moe_final(hs, g_ffn[1], w_router[0], w_gu, w_dn, g_final, tm=128)

    st_p = jnp.stack([jnp.stack([st_p[b, h * GLA_HDV:(h + 1) * GLA_HDV, h * GLA_HDK:(h + 1) * GLA_HDK].T
                                 for h in range(GLA_HEADS)]) for b in range(B)])
    return (y_p.reshape(B, T, D), y_s.reshape(Bs, 1, D),
            k_p.reshape(1, B, T, SB_HEADS, HEAD_DIM), v_p.reshape(1, B, T, SB_HEADS, HEAD_DIM),
            k_s.reshape(1, Bs, 1, SB_HEADS, HEAD_DIM), v_s.reshape(1, Bs, 1, SB_HEADS, HEAD_DIM),
            st_p[None], st_s.reshape(1, Bs, GLA_HEADS, GLA_HDK, GLA_HDV),
            jnp.stack([mk0, mk1]), jnp.stack([mv0, mv1]))
```

```python
import functools

import jax
import jax.numpy as jnp
from jax import lax
from jax.experimental import pallas as pl
from jax.experimental.pallas import tpu as pltpu

F32 = jnp.float32
BF16 = jnp.bfloat16

D_MODEL = 1024
HEAD_DIM = 64
X_HEADS = 4
X_W = X_HEADS * HEAD_DIM
TOK_W = D_MODEL - X_W
SB_HEADS = TOK_W // HEAD_DIM
GLA_HEADS = 4
GLA_DV = TOK_W
GLA_DK = GLA_DV // 2
GLA_HDK = GLA_DK // GLA_HEADS
GLA_HDV = GLA_DV // GLA_HEADS
GLA_GATE_RANK = 16
GLA_TAU = 16.0
GLA_CHUNK = 64
GLA_SUB = 16
N_EXPERTS = 8
RMS_EPS = 1e-6
PAGE_SIZE = 128

LANES = 128
SB_KB = 256
VMEM_LIMIT = 56 * 1024 * 1024


def _cparams(sem):
    return pltpu.CompilerParams(dimension_semantics=sem, vmem_limit_bytes=VMEM_LIMIT)


def _rms(x, g):
    return x * lax.rsqrt(jnp.mean(x * x, axis=-1, keepdims=True) + RMS_EPS) * g


def _dot(a, b):
    return jnp.dot(a, b, preferred_element_type=F32)


def _split_bf16(x):
    hi = x.astype(BF16)
    lo = (x - hi.astype(F32)).astype(BF16)
    return hi, lo


def _dot_split(a, b):
    ah, al = _split_bf16(a)
    bh, bl = _split_bf16(b)
    return _dot(ah, bh) + (_dot(al, bh) + _dot(ah, bl))


def _matmul(a, w):
    if w.dtype == BF16:
        return _dot(a.astype(BF16), w)
    return _dot_split(a, w)


def _dot_nt(a, b):
    return lax.dot_general(a, b, (((1,), (1,)), ((), ())), preferred_element_type=F32)


def _dot_tn(a, b):
    return lax.dot_general(a, b, (((0,), (0,)), ((), ())), preferred_element_type=F32)


def _softplus(z):
    return jnp.maximum(z, 0.0) + jnp.log(1.0 + jnp.exp(-jnp.abs(z)))


def _tile(n, pref):
    t = min(n, pref)
    assert n % t == 0, (n, t)
    return t


def _norm_matmul_kernel(x_ref, g_ref, w_ref, o_ref):
    o_ref[...] = _matmul(_rms(x_ref[...], g_ref[...]), w_ref[...])


def norm_matmul(x, g, w_bf16, tm=512):
    n, d = x.shape
    m = w_bf16.shape[1]
    tm = _tile(n, tm)
    return pl.pallas_call(
        _norm_matmul_kernel,
        grid=(n // tm,),
        in_specs=[pl.BlockSpec((tm, d), lambda i: (i, 0)),
                  pl.BlockSpec((1, d), lambda i: (0, 0)),
                  pl.BlockSpec((d, m), lambda i: (0, 0))],
        out_specs=pl.BlockSpec((tm, m), lambda i: (i, 0)),
        out_shape=jax.ShapeDtypeStruct((n, m), F32),
        compiler_params=_cparams(("parallel",)),
        name="norm_matmul",
    )(x, g.reshape(1, d), w_bf16)


def _sb_inproj_kernel(x_ref, g_ref, w_ref, qt_ref, kh_ref, vt_ref, k_ref, v_ref, xq_ref):
    tm = x_ref.shape[0]
    xn = _rms(x_ref[...], g_ref[...]).astype(BF16)
    z = _dot(xn, w_ref[...])
    k_ref[...] = z[:, TOK_W:2 * TOK_W]
    v_ref[...] = z[:, 2 * TOK_W:3 * TOK_W]
    xq_ref[...] = z[:, 3 * TOK_W:]
    for h in range(SB_HEADS):
        lo = TOK_W + h * HEAD_DIM
        kh_ref[h] = z[:, lo:lo + HEAD_DIM].astype(BF16)
    for p in range(SB_HEADS // 2):
        qt2 = (z[:, p * LANES:(p + 1) * LANES] * (HEAD_DIM ** -0.5)).T
        vt2 = z[:, 2 * TOK_W + p * LANES:2 * TOK_W + (p + 1) * LANES].T
        for hh in range(2):
            rows = slice(hh * HEAD_DIM, (hh + 1) * HEAD_DIM)
            for c in range(tm // SB_KB):
                cols = slice(c * SB_KB, (c + 1) * SB_KB)
                qt_ref[2 * p + hh, c] = qt2[rows, cols].astype(BF16)
                vt_ref[2 * p + hh, c] = vt2[rows, cols].astype(BF16)


def sb_inproj(x, g, w_bf16, tm=512):
    n, d = x.shape
    m = w_bf16.shape[1]
    tm = _tile(n, tm)
    nkb = tm // SB_KB
    t_spec = pl.BlockSpec((SB_HEADS, nkb, HEAD_DIM, SB_KB), lambda i: (0, i, 0, 0))
    t_shape = jax.ShapeDtypeStruct((SB_HEADS, n // SB_KB, HEAD_DIM, SB_KB), BF16)
    return pl.pallas_call(
        _sb_inproj_kernel,
        grid=(n // tm,),
        in_specs=[pl.BlockSpec((tm, d), lambda i: (i, 0)),
                  pl.BlockSpec((1, d), lambda i: (0, 0)),
                  pl.BlockSpec((d, m), lambda i: (0, 0))],
        out_specs=[t_spec,
                   pl.BlockSpec((SB_HEADS, tm, HEAD_DIM), lambda i: (0, i, 0)),
                   t_spec,
                   pl.BlockSpec((tm, TOK_W), lambda i: (i, 0)),
                   pl.BlockSpec((tm, TOK_W), lambda i: (i, 0)),
                   pl.BlockSpec((tm, X_W), lambda i: (i, 0))],
        out_shape=[t_shape,
                   jax.ShapeDtypeStruct((SB_HEADS, n, HEAD_DIM), BF16),
                   t_shape,
                   jax.ShapeDtypeStruct((n, TOK_W), F32),
                   jax.ShapeDtypeStruct((n, TOK_W), F32),
                   jax.ShapeDtypeStruct((n, X_W), F32)],
        compiler_params=_cparams(("parallel",)),
        name="sb_inproj",
    )(x, g.reshape(1, d), w_bf16)


LOG2E = 1.4426950408889634


def _sb_prompt_kernel(bias_ref, qt_ref, k_ref, vt_ref, o_ref, z_sc, t_sc, sp_sc, a_sc):
    hp = pl.program_id(1)
    i = pl.program_id(2)
    tq = qt_ref.shape[-1]
    row = lax.broadcasted_iota(jnp.int32, (SB_KB, SB_KB), 0)
    col = lax.broadcasted_iota(jnp.int32, (SB_KB, SB_KB), 1)
    tri = (col > row).astype(BF16)
    causal = row < col
    heads = range(2)

    def scores(j, hh):
        k = k_ref[hh, pl.ds(pl.multiple_of(j * SB_KB, SB_KB), SB_KB), :]
        return _dot(k, qt_ref[hh, 0]) + bias_ref[2 * hp + hh]

    def stage1(z, hh, slot, mask):
        e = jnp.exp2(jnp.abs(z) * (-LOG2E))
        sp = jnp.maximum(z, 0.0) + jnp.log(1.0 + e)
        t = z - sp
        if mask is not None:
            sp = jnp.where(mask, sp, 0.0)
            t = jnp.where(mask, t, -jnp.inf)
        t_sc[slot, hh] = t
        sp_sc[hh] = sp.astype(BF16)
        return sp[0:8, :].astype(BF16).astype(F32)

    def tick(j1, slot, state, with_stage1):
        heads8, ws, carries, accs = state
        j3 = jnp.minimum(j1 + 2, i)
        pvs = [_dot(vt_ref[hh, j3], a_sc[hh]) for hh in heads]
        locs = [_dot(tri, sp_sc[hh]) for hh in heads]
        heads8_next = heads8
        if with_stage1:
            zs_next = [scores(jnp.maximum(j1 - 1, 0), hh) for hh in heads]
            heads8_next = [stage1(z_sc[slot, hh], hh, 1 - slot, None) for hh in heads]
            for hh in heads:
                z_sc[1 - slot, hh] = zs_next[hh]
        ws_next, new_c, new_a = [], [], []
        for hh in heads:
            a_sc[hh] = jnp.exp(t_sc[slot, hh] - locs[hh]).astype(BF16)
            ws_next.append(jnp.exp(-carries[hh]))
            total = locs[hh][0:1, :] + heads8[hh][0:1, :]
            new_c.append(carries[hh] + jnp.broadcast_to(total, carries[hh].shape))
            new_a.append(accs[hh] + ws[hh][0:1, :] * pvs[hh])
        return heads8_next, ws_next, new_c, new_a

    heads8 = [stage1(scores(i, hh), hh, 0, causal) for hh in heads]
    for hh in heads:
        z_sc[0, hh] = scores(jnp.maximum(i - 1, 0), hh)
        a_sc[hh] = jnp.zeros((SB_KB, tq), BF16)
    zrow = [jnp.minimum(h8, 0.0) for h8 in heads8]
    state = (heads8, zrow, zrow, [jnp.zeros((HEAD_DIM, tq), F32) for _ in heads])
    state = lax.fori_loop(0, i, lambda jj, st: tick(i - 1 - jj, jj & 1, st, True), state)
    _, ws, _, accs = tick(-1, i & 1, state, False)
    accs = [accs[hh] + ws[hh][0:1, :] * _dot(vt_ref[hh, 0], a_sc[hh]) for hh in heads]
    o_ref[...] = jnp.concatenate(accs, axis=0).T


def sb_prompt_attention(bias, qt, kh, vt, batch):
    n = kh.shape[1]
    t = n // batch
    tq = SB_KB
    nq = t // tq
    return pl.pallas_call(
        _sb_prompt_kernel,
        grid=(batch, SB_HEADS // 2, nq),
        in_specs=[pl.BlockSpec(memory_space=pltpu.SMEM),
                  pl.BlockSpec((2, 1, HEAD_DIM, tq), lambda b, h, i: (h, b * nq + i, 0, 0)),
                  pl.BlockSpec((2, t, HEAD_DIM), lambda b, h, i: (h, b, 0)),
                  pl.BlockSpec((2, nq, HEAD_DIM, SB_KB), lambda b, h, i: (h, b, 0, 0))],
        out_specs=pl.BlockSpec((tq, 2 * HEAD_DIM), lambda b, h, i: (b * nq + i, h)),
        out_shape=jax.ShapeDtypeStruct((n, TOK_W), F32),
        scratch_shapes=[pltpu.VMEM((2, 2, SB_KB, tq), F32), pltpu.VMEM((2, 2, SB_KB, tq), F32),
                        pltpu.VMEM((2, SB_KB, tq), BF16), pltpu.VMEM((2, SB_KB, tq), BF16)],
        compiler_params=_cparams(("parallel", "parallel", "arbitrary")),
        name="sb_prompt_attention",
    )(bias, qt, kh, vt)


SB_PAGES_PER_STEP = 8


def _split3_bf16(x):
    hi = x.astype(BF16)
    r = x - hi.astype(F32)
    mid = r.astype(BF16)
    lo = (r - mid.astype(F32)).astype(BF16)
    return hi, mid, lo


def _sb_sample_kernel(pt_ref, q_ref, bias_ref, *refs):
    del pt_ref
    npg = (len(refs) - 3) // 2
    k_refs, v_refs = refs[:npg], refs[npg:2 * npg]
    o_ref, carry_sc, acc_sc = refs[2 * npg:]
    g = pl.program_id(1)
    hpad = carry_sc.shape[0]

    @pl.when(g == 0)
    def _():
        carry_sc[...] = jnp.zeros_like(carry_sc)
        acc_sc[...] = jnp.zeros_like(acc_sc)

    r = lax.broadcasted_iota(jnp.int32, (PAGE_SIZE, PAGE_SIZE), 0)
    c = lax.broadcasted_iota(jnp.int32, (PAGE_SIZE, PAGE_SIZE), 1)
    tri = (r > c).astype(BF16)
    qs = [_row_to_col(q_ref[0, h:h + 1, :]) * (HEAD_DIM ** -0.5) for h in range(SB_HEADS)]
    pad = jnp.zeros((hpad - SB_HEADS, PAGE_SIZE), F32)
    carry = carry_sc[...]
    for p in range(npg):
        rows = [jnp.sum(k_refs[p][0, h] * qs[h], axis=0, keepdims=True) for h in range(SB_HEADS)]
        z = jnp.concatenate(rows + [pad], axis=0) + bias_ref[...]
        sp = _softplus(z)
        hi, mid, lo = _split3_bf16(sp)
        loc = _dot(hi, tri) + (_dot(mid, tri) + _dot(lo, tri))
        a = jnp.exp(z - sp - loc - carry)
        for h in range(SB_HEADS):
            acc_sc[h] += a[h:h + 1, :] * v_refs[p][0, h]
        carry = carry + jnp.broadcast_to(loc[:, 0:1] + sp[:, 0:1], carry.shape)
    carry_sc[...] = carry

    @pl.when(g == pl.num_programs(1) - 1)
    def _():
        o_ref[0] = jnp.concatenate(
            [_col_to_row(jnp.sum(acc_sc[h], axis=-1, keepdims=True)) for h in range(SB_HEADS)], axis=0)


def sb_sample_attention(q, bias, cache_k, cache_v, page_table):
    bs = q.shape[0]
    n_pages = page_table.shape[1]
    npg = _tile(n_pages, SB_PAGES_PER_STEP)
    hpad = 16
    bias_b = jnp.zeros((hpad, PAGE_SIZE), F32).at[:SB_HEADS].set(
        jnp.broadcast_to(bias[:, None], (SB_HEADS, PAGE_SIZE)))
    pt = page_table.reshape(-1)

    def page_spec(p):
        def index_map(b, g, pt_ref):
            return (pt_ref[b * n_pages + (n_pages - 1 - g * npg - p)], 0, 0, 0)
        return pl.BlockSpec((1, SB_HEADS, HEAD_DIM, PAGE_SIZE), index_map)

    qo_spec = pl.BlockSpec((1, SB_HEADS, HEAD_DIM), lambda b, g, pt_ref: (b, 0, 0))
    grid_spec = pltpu.PrefetchScalarGridSpec(
        num_scalar_prefetch=1,
        grid=(bs, n_pages // npg),
        in_specs=[qo_spec, pl.BlockSpec((hpad, PAGE_SIZE), lambda b, g, pt_ref: (0, 0))]
        + [page_spec(p) for p in range(npg)] * 2,
        out_specs=qo_spec,
        scratch_shapes=[pltpu.VMEM((hpad, PAGE_SIZE), F32), pltpu.VMEM((SB_HEADS, HEAD_DIM, PAGE_SIZE), F32)],
    )
    out = pl.pallas_call(
        _sb_sample_kernel,
        grid_spec=grid_spec,
        out_shape=jax.ShapeDtypeStruct((bs, SB_HEADS, HEAD_DIM), F32),
        compiler_params=_cparams(("parallel", "arbitrary")),
        name="sb_sample_attention",
    )(pt, q.reshape(bs, SB_HEADS, HEAD_DIM), bias_b, *([cache_k] * npg), *([cache_v] * npg))
    return out.reshape(bs, TOK_W)


def _softmax_rows(s):
    m = jnp.max(s, axis=-1, keepdims=True)
    e = jnp.exp(s - m)
    return e / jnp.sum(e, axis=-1, keepdims=True)


def _cross_prompt_kernel(q_ref, mkt_ref, mv_ref, o_ref):
    q = q_ref[...]
    for h in range(X_HEADS):
        lo = h * HEAD_DIM
        qh = q[:, lo:lo + HEAD_DIM].astype(BF16)
        s = _dot(qh, mkt_ref[0, h]) * (HEAD_DIM ** -0.5)
        p = _softmax_rows(s).astype(BF16)
        o_ref[:, lo:lo + HEAD_DIM] = _dot(p, mv_ref[0, h])


def cross_prompt(xq, mkt, mvh, batch, tq=512):
    n = xq.shape[0]
    t = n // batch
    tq = _tile(t, tq)
    nq = t // tq
    nm = mkt.shape[-1]
    return pl.pallas_call(
        _cross_prompt_kernel,
        grid=(batch, nq),
        in_specs=[pl.BlockSpec((tq, X_W), lambda b, i: (b * nq + i, 0)),
                  pl.BlockSpec((1, X_HEADS, HEAD_DIM, nm), lambda b, i: (b, 0, 0, 0)),
                  pl.BlockSpec((1, X_HEADS, nm, HEAD_DIM), lambda b, i: (b, 0, 0, 0))],
        out_specs=pl.BlockSpec((tq, X_W), lambda b, i: (b * nq + i, 0)),
        out_shape=jax.ShapeDtypeStruct((n, X_W), F32),
        compiler_params=_cparams(("parallel", "parallel")),
        name="cross_prompt",
    )(xq, mkt, mvh)


def _eye(n):
    return lax.broadcasted_iota(jnp.int32, (n, n), 0) == lax.broadcasted_iota(jnp.int32, (n, n), 1)


def _row_to_col(row):
    n = row.shape[1]
    return jnp.sum(jnp.where(_eye(n), jnp.broadcast_to(row, (n, n)), 0.0), axis=1, keepdims=True)


def _col_to_row(col):
    n = col.shape[0]
    return jnp.sum(jnp.where(_eye(n), jnp.broadcast_to(col, (n, n)), 0.0), axis=0, keepdims=True)


def _cross_sample_kernel(q_ref, mk_ref, mv_ref, o_ref):
    for j in range(q_ref.shape[0]):
        rows = []
        for h in range(X_HEADS):
            q = _row_to_col(q_ref[j, h:h + 1, :])
            s = jnp.sum(mk_ref[0, j, h] * q, axis=0, keepdims=True) * (HEAD_DIM ** -0.5)
            p = _softmax_rows(s)
            rows.append(_col_to_row(jnp.sum(mv_ref[0, j, h] * p, axis=-1, keepdims=True)))
        o_ref[j] = jnp.concatenate(rows, axis=0)


def cross_sample(xq, mk, mv, layer, g=8):
    bs = xq.shape[0]
    g = _tile(bs, g)
    nm = mk.shape[-1]
    qo_spec = pl.BlockSpec((g, X_HEADS, HEAD_DIM), lambda i: (i, 0, 0))
    m_spec = pl.BlockSpec((1, g, X_HEADS, HEAD_DIM, nm), lambda i: (layer, i, 0, 0, 0))
    out = pl.pallas_call(
        _cross_sample_kernel,
        grid=(bs // g,),
        in_specs=[qo_spec, m_spec, m_spec],
        out_specs=qo_spec,
        out_shape=jax.ShapeDtypeStruct((bs, X_HEADS, HEAD_DIM), F32),
        compiler_params=_cparams(("parallel",)),
        name="cross_sample",
    )(xq.reshape(bs, X_HEADS, HEAD_DIM), mk, mv)
    return out.reshape(bs, X_W)


def _outproj_kernel(tok_ref, xa_ref, h_ref, wt_ref, wb_ref, o_ref):
    o_ref[...] = h_ref[...] + _matmul(tok_ref[...], wt_ref[...]) + _matmul(xa_ref[...], wb_ref[...])


def outproj(tok, xa, h, w_bf16, tm=512):
    n, d = h.shape
    tm = _tile(n, tm)
    wt, wb = w_bf16[:TOK_W], w_bf16[TOK_W:]
    return pl.pallas_call(
        _outproj_kernel,
        grid=(n // tm,),
        in_specs=[pl.BlockSpec((tm, TOK_W), lambda i: (i, 0)),
                  pl.BlockSpec((tm, X_W), lambda i: (i, 0)),
                  pl.BlockSpec((tm, d), lambda i: (i, 0)),
                  pl.BlockSpec((TOK_W, d), lambda i: (0, 0)),
                  pl.BlockSpec((X_W, d), lambda i: (0, 0))],
        out_specs=pl.BlockSpec((tm, d), lambda i: (i, 0)),
        out_shape=jax.ShapeDtypeStruct((n, d), F32),
        compiler_params=_cparams(("parallel",)),
        name="outproj",
    )(tok, xa, h, wt, wb)


def _ffn_kernel(x_ref, g_ref, wg_ref, wu_ref, wd_ref, o_ref, xn_sc, acc_sc):
    f = pl.program_id(1)

    @pl.when(f == 0)
    def _():
        xn_sc[...] = _rms(x_ref[...], g_ref[...]).astype(xn_sc.dtype)
        acc_sc[...] = jnp.zeros_like(acc_sc)

    xn = xn_sc[...]
    gate = _matmul(xn, wg_ref[...])
    up = _matmul(xn, wu_ref[...])
    acc_sc[...] += _matmul(jax.nn.silu(gate) * up, wd_ref[...])

    @pl.when(f == pl.num_programs(1) - 1)
    def _():
        o_ref[...] = x_ref[...] + acc_sc[...]


def ffn(x, g, w_gu, w_down, tm=1024, tf=256):
    n, d = x.shape
    dff = w_down.shape[0]
    tm = _tile(n, tm)
    nf = dff // tf
    return pl.pallas_call(
        _ffn_kernel,
        grid=(n // tm, nf),
        in_specs=[pl.BlockSpec((tm, d), lambda i, f: (i, 0)),
                  pl.BlockSpec((1, d), lambda i, f: (0, 0)),
                  pl.BlockSpec((d, tf), lambda i, f: (0, f)),
                  pl.BlockSpec((d, tf), lambda i, f: (0, f + nf)),
                  pl.BlockSpec((tf, d), lambda i, f: (f, 0))],
        out_specs=pl.BlockSpec((tm, d), lambda i, f: (i, 0)),
        out_shape=jax.ShapeDtypeStruct((n, d), F32),
        scratch_shapes=[pltpu.VMEM((tm, d), w_gu.dtype), pltpu.VMEM((tm, d), F32)],
        compiler_params=_cparams(("parallel", "arbitrary")),
        name="ffn",
    )(x, g.reshape(1, d), w_gu, w_gu, w_down)


def _router_kernel(x_ref, g_ref, w_ref, o_ref, cnt_ref, cnt_sc):
    @pl.when(pl.program_id(0) == 0)
    def _():
        cnt_sc[...] = jnp.zeros_like(cnt_sc)

    xn = _rms(x_ref[...], g_ref[...])
    logits = _dot_split(xn, w_ref[...])
    lane = lax.broadcasted_iota(jnp.int32, logits.shape, 1)
    neg = jnp.float32(-jnp.inf)
    logits = jnp.where(lane < N_EXPERTS, logits, neg)
    m1 = jnp.max(logits, axis=-1, keepdims=True)
    i1 = jnp.min(jnp.where(logits == m1, lane, LANES), axis=-1, keepdims=True)
    rest = jnp.where(lane == i1, neg, logits)
    m2 = jnp.max(rest, axis=-1, keepdims=True)
    i2 = jnp.min(jnp.where(rest == m2, lane, LANES), axis=-1, keepdims=True)
    e2 = jnp.exp(m2 - m1)
    den = 1.0 + e2
    oh1 = lane == i1
    oh2 = lane == i2
    cnt = (oh1 | oh2).astype(BF16)
    tm = cnt.shape[0]
    r = lax.broadcasted_iota(jnp.int32, (tm, tm), 0)
    c = lax.broadcasted_iota(jnp.int32, (tm, tm), 1)
    before = _dot((c < r).astype(BF16), cnt) + cnt_sc[...]
    r1 = jnp.sum(jnp.where(oh1, before, 0.0), axis=-1, keepdims=True)
    r2 = jnp.sum(jnp.where(oh2, before, 0.0), axis=-1, keepdims=True)
    cnt_sc[...] = before[tm - 1:tm, :] + cnt[tm - 1:tm, :].astype(F32)
    cnt_ref[...] = cnt_sc[...]
    info = jnp.zeros(logits.shape, F32)
    for k, val in enumerate((i1.astype(F32), i2.astype(F32), 1.0 / den, e2 / den, r1, r2)):
        info = jnp.where(lane == k, val, info)
    o_ref[...] = info


R_E1, R_E2, R_G1, R_G2, R_R1, R_R2 = range(6)


def router(x, g, w_router, tm=512):
    n, d = x.shape
    tm = _tile(n, tm)
    w_pad = jnp.zeros((d, LANES), F32).at[:, :N_EXPERTS].set(w_router)
    return pl.pallas_call(
        _router_kernel,
        grid=(n // tm,),
        in_specs=[pl.BlockSpec((tm, d), lambda i: (i, 0)),
                  pl.BlockSpec((1, d), lambda i: (0, 0)),
                  pl.BlockSpec((d, LANES), lambda i: (0, 0))],
        out_specs=[pl.BlockSpec((tm, LANES), lambda i: (i, 0)),
                   pl.BlockSpec((1, LANES), lambda i: (0, 0))],
        out_shape=[jax.ShapeDtypeStruct((n, LANES), F32), jax.ShapeDtypeStruct((1, LANES), F32)],
        scratch_shapes=[pltpu.VMEM((1, LANES), F32)],
        compiler_params=_cparams(("arbitrary",)),
        name="router",
    )(x, g.reshape(1, d), w_pad)


def _slot_sources_kernel(s1_ref, s2_ref, src_ref):
    def clear(s, carry):
        src_ref[s] = 0
        return carry

    def put(t, carry):
        src_ref[s1_ref[t]] = t
        src_ref[s2_ref[t]] = t
        return carry

    lax.fori_loop(0, src_ref.shape[0], clear, 0, unroll=8)
    lax.fori_loop(0, s1_ref.shape[0], put, 0, unroll=8)


def slot_sources(slot1, slot2, n_slots):
    return pl.pallas_call(
        _slot_sources_kernel,
        in_specs=[pl.BlockSpec(memory_space=pltpu.SMEM), pl.BlockSpec(memory_space=pltpu.SMEM)],
        out_specs=pl.BlockSpec(memory_space=pltpu.SMEM),
        out_shape=jax.ShapeDtypeStruct((n_slots,), jnp.int32),
        name="slot_sources",
    )(slot1, slot2)


def _row_copy(src_hbm, src_row, dst_ref, dst_row, sem):
    return pltpu.make_async_copy(src_hbm.at[pl.ds(src_row, 1)], dst_ref.at[pl.ds(dst_row, 1)], sem)


def _expert_kernel(te_ref, nv_ref, src_ref, x_hbm, g_ref, wg_ref, wu_ref, wd_ref, o_ref, xbuf, xn_sc, acc_sc, sem):
    del te_ref
    t = pl.program_id(0)
    f = pl.program_id(1)
    nf = pl.num_programs(1)
    tm = xbuf.shape[1]
    n_valid = nv_ref[0]
    half = t % 2

    def fetch(tile, first_row, n_rows, dst_half):
        for j in range(n_rows):
            r = first_row + j
            _row_copy(x_hbm, src_ref[tile * tm + r], xbuf.at[dst_half], r, sem.at[dst_half]).start()

    @pl.when((t == 0) & (f == 0))
    def _():
        fetch(0, 0, tm, 0)

    @pl.when((f == 0) & (t <= n_valid))
    def _():
        def wait(r, carry):
            _row_copy(x_hbm, 0, xbuf.at[half], r, sem.at[half]).wait()
            return carry
        lax.fori_loop(0, tm, wait, 0, unroll=8)

    @pl.when(t < n_valid)
    def _():
        part = tm // nf
        fetch(t + 1, f * part, part, 1 - half)

        @pl.when(f == 0)
        def _():
            xn_sc[...] = _rms(xbuf[half], g_ref[...]).astype(BF16)
            acc_sc[...] = jnp.zeros_like(acc_sc)

        xn = xn_sc[...]
        gate = _dot(xn, wg_ref[0])
        up = _dot(xn, wu_ref[0])
        act = (jax.nn.silu(gate) * up).astype(BF16)
        acc_sc[...] += _dot(act, wd_ref[0])

        @pl.when(f == nf - 1)
        def _():
            o_ref[...] = acc_sc[...]

    @pl.when((t >= n_valid) & (f == nf - 1))
    def _():
        o_ref[...] = jnp.zeros_like(o_ref)


def expert_swiglu(x, g, src, tile_expert, n_valid, w_gu_bf16, w_down_bf16, tm, tf=896):
    n_slots = src.shape[0]
    d = x.shape[1]
    dff = w_down_bf16.shape[1]
    nf = dff // tf
    assert tm % nf == 0

    def wmap(off):
        def index_map(t, f, te_ref, nv_ref, src_ref):
            return (te_ref[t], 0, jnp.where(t < nv_ref[0], f, nf - 1) + off)
        return index_map

    def dmap(t, f, te_ref, nv_ref, src_ref):
        return (te_ref[t], jnp.where(t < nv_ref[0], f, nf - 1), 0)

    grid_spec = pltpu.PrefetchScalarGridSpec(
        num_scalar_prefetch=3,
        grid=(n_slots // tm, nf),
        in_specs=[pl.BlockSpec(memory_space=pl.ANY),
                  pl.BlockSpec((1, d), lambda t, f, te_ref, nv_ref, src_ref: (0, 0)),
                  pl.BlockSpec((1, d, tf), wmap(0)),
                  pl.BlockSpec((1, d, tf), wmap(nf)),
                  pl.BlockSpec((1, tf, d), dmap)],
        out_specs=pl.BlockSpec((tm, d), lambda t, f, te_ref, nv_ref, src_ref: (t, 0)),
        scratch_shapes=[pltpu.VMEM((2, tm, d), F32), pltpu.VMEM((tm, d), BF16), pltpu.VMEM((tm, d), F32),
                        pltpu.SemaphoreType.DMA((2,))],
    )
    return pl.pallas_call(
        _expert_kernel,
        grid_spec=grid_spec,
        out_shape=jax.ShapeDtypeStruct((n_slots, d), F32),
        compiler_params=_cparams(("arbitrary", "arbitrary")),
        name="expert_swiglu",
    )(tile_expert, n_valid, src, x, g.reshape(1, d), w_gu_bf16, w_gu_bf16, w_down_bf16)


def _combine_kernel(s1_ref, s2_ref, x_ref, info_ref, gf_ref, ys_hbm, o_ref, buf, sem):
    tc = x_ref.shape[0]
    base = pl.program_id(0) * tc

    def start(r, carry):
        _row_copy(ys_hbm, s1_ref[base + r], buf.at[0], r, sem).start()
        _row_copy(ys_hbm, s2_ref[base + r], buf.at[1], r, sem).start()
        return carry

    def wait(r, carry):
        _row_copy(ys_hbm, 0, buf.at[0], r, sem).wait()
        _row_copy(ys_hbm, 0, buf.at[1], r, sem).wait()
        return carry

    lax.fori_loop(0, tc, start, 0, unroll=8)
    lax.fori_loop(0, tc, wait, 0, unroll=8)
    info = info_ref[...]
    g1 = info[:, R_G1:R_G1 + 1]
    g2 = info[:, R_G2:R_G2 + 1]
    o_ref[...] = _rms(x_ref[...] + (g1 * buf[0] + g2 * buf[1]), gf_ref[...])


def combine_final(x, info, slot1, slot2, ys, g_final, tc=256):
    n, d = x.shape
    tc = _tile(n, tc)
    grid_spec = pltpu.PrefetchScalarGridSpec(
        num_scalar_prefetch=2,
        grid=(n // tc,),
        in_specs=[pl.BlockSpec((tc, d), lambda i, s1, s2: (i, 0)),
                  pl.BlockSpec((tc, LANES), lambda i, s1, s2: (i, 0)),
                  pl.BlockSpec((1, d), lambda i, s1, s2: (0, 0)),
                  pl.BlockSpec(memory_space=pl.ANY)],
        out_specs=pl.BlockSpec((tc, d), lambda i, s1, s2: (i, 0)),
        scratch_shapes=[pltpu.VMEM((2, tc, d), F32), pltpu.SemaphoreType.DMA(())],
    )
    return pl.pallas_call(
        _combine_kernel,
        grid_spec=grid_spec,
        out_shape=jax.ShapeDtypeStruct((n, d), F32),
        compiler_params=_cparams(("arbitrary",)),
        name="combine_final",
    )(slot1, slot2, x, info, g_final.reshape(1, d), ys)


def moe_final(x, g, w_router, w_gu_bf16, w_down_bf16, g_final, tm):
    n, d = x.shape
    info, counts = router(x, g, w_router)
    counts = counts[0, :N_EXPERTS].astype(jnp.int32)
    padded = ((counts + tm - 1) // tm) * tm
    ends = jnp.cumsum(padded)
    offs = ends - padded
    n_slots = (-(-2 * n // tm) + N_EXPERTS) * tm
    e1 = info[:, R_E1].astype(jnp.int32)
    e2 = info[:, R_E2].astype(jnp.int32)
    slot1 = offs[e1] + info[:, R_R1].astype(jnp.int32)
    slot2 = offs[e2] + info[:, R_R2].astype(jnp.int32)
    src = slot_sources(slot1, slot2, n_slots)
    tile_start = jnp.arange(n_slots // tm, dtype=jnp.int32) * tm
    n_valid = (ends[-1] // tm).reshape(1)
    tile_expert = jnp.minimum(jnp.sum(tile_start[:, None] >= ends[None, :], axis=1), N_EXPERTS - 1).astype(jnp.int32)
    tile_expert = jnp.where(tile_start < ends[-1], tile_expert, tile_expert[jnp.maximum(n_valid[0] - 1, 0)])
    ys = expert_swiglu(x, g, src, tile_expert, n_valid, w_gu_bf16, w_down_bf16, tm)
    return combine_final(x, info, slot1, slot2, ys, g_final)


GLA_R_OFF = 2 * GLA_DK + GLA_DV
GLA_XQ_OFF2 = GLA_R_OFF + GLA_DV
GLA_LR_OFF2 = GLA_XQ_OFF2 + X_W


def _head_of(col, width):
    return ((col >= width).astype(jnp.int32) + (col >= 2 * width).astype(jnp.int32)
            + (col >= 3 * width).astype(jnp.int32))


def _gla_log_decay(lr, wgu, bg):
    gl = _matmul(lr, wgu) + bg
    return -_softplus(-gl) / GLA_TAU


def _gla_out(o, r, gon):
    col = lax.broadcasted_iota(jnp.int32, (1, GLA_DV), 1)
    hv = _head_of(col, GLA_HDV)
    o2 = o * o
    inv = jnp.zeros_like(o)
    for h in range(GLA_HEADS):
        m = hv == h
        ms = jnp.sum(jnp.where(m, o2, 0.0), axis=-1, keepdims=True) * (1.0 / GLA_HDV)
        inv = jnp.where(m, lax.rsqrt(ms + RMS_EPS), inv)
    return (o * inv * gon) * jax.nn.silu(r)


def _gla_prompt_kernel(z_ref, wgu_ref, bg_ref, gon_ref, bd_ref, tok_ref, st_ref, st_sc):
    nb = z_ref.shape[0]
    gt = z_ref.shape[1]
    c = GLA_CHUNK

    @pl.when(pl.program_id(0) == 0)
    def _():
        st_sc[...] = jnp.zeros_like(st_sc)

    rr = lax.broadcasted_iota(jnp.int32, (c, c), 0)
    cc = lax.broadcasted_iota(jnp.int32, (c, c), 1)
    ltri = (cc <= rr).astype(BF16)
    r4 = lax.broadcasted_iota(jnp.int32, (c, GLA_HEADS * c), 0)
    c4 = lax.broadcasted_iota(jnp.int32, (c, GLA_HEADS * c), 1)
    intra_mask = (c4 & (c - 1)) <= r4
    key_row = lax.broadcasted_iota(jnp.int32, (c, 1), 0)
    hk = _head_of(lax.broadcasted_iota(jnp.int32, (1, GLA_DK), 1), GLA_HDK)
    hv = _head_of(lax.broadcasted_iota(jnp.int32, (1, GLA_DV), 1), GLA_HDV)
    wgu = wgu_ref[...]
    bg = bg_ref[...]
    gon = gon_ref[...]
    bd = bd_ref[...]

    for ci in range(gt // c):
        for b in range(nb):
            zc = z_ref[b, ci * c:(ci + 1) * c, :]
            q = zc[:, 0:GLA_DK] * (GLA_HDK ** -0.5)
            k = zc[:, GLA_DK:2 * GLA_DK]
            v = zc[:, 2 * GLA_DK:GLA_R_OFF]
            r = zc[:, GLA_R_OFF:GLA_XQ_OFF2]
            lr = zc[:, GLA_LR_OFF2:GLA_LR_OFF2 + GLA_GATE_RANK]
            la = _gla_log_decay(lr, wgu, bg)
            hi = la.astype(BF16)
            r1 = la - hi.astype(F32)
            mid = r1.astype(BF16)
            low = (r1 - mid.astype(F32)).astype(BF16)
            bc = _dot(ltri, hi) + (_dot(ltri, mid) + _dot(ltri, low))
            qe = (q * jnp.exp(bc)).astype(BF16)
            vb = v.astype(BF16)
            v_stack = jnp.concatenate(
                [jnp.where(hv == h, vb, jnp.zeros_like(vb)) for h in range(GLA_HEADS)], axis=0)
            parts = []
            for g0 in range(0, c, GLA_SUB):
                ref = bc[g0 - 1:g0, :] if g0 else jnp.zeros((1, GLA_DK), F32)
                qg = (q[g0:g0 + GLA_SUB] * jnp.exp(bc[g0:g0 + GLA_SUB] - ref)).astype(BF16)
                kg = (k * jnp.exp(jnp.where(key_row < g0 + GLA_SUB, ref - bc, -jnp.inf))).astype(BF16)
                kg_stack = jnp.concatenate(
                    [jnp.where(hk == h, kg, jnp.zeros_like(kg)) for h in range(GLA_HEADS)], axis=0)
                parts.append(_dot_nt(qg, kg_stack))
            scores = jnp.where(intra_mask, jnp.concatenate(parts, axis=0), 0.0)
            o_intra = _dot(scores.astype(BF16), v_stack)
            st = st_sc[b]
            o_inter = _dot_nt(qe, st.astype(BF16))
            tok_ref[b, ci * c:(ci + 1) * c, :] = _gla_out(o_inter + o_intra, r, gon)
            b_last = bc[c - 1:c, :]
            kd = (k * jnp.exp(b_last - bc)).astype(BF16)
            st_sc[b] = st * jnp.exp(b_last) + _dot_tn(vb, kd) * bd

    @pl.when(pl.program_id(0) == pl.num_programs(0) - 1)
    def _():
        st_ref[...] = st_sc[...]


def gla_prompt(z, w_gate_up, b_gate, g_onorm, gt=256):
    nb, t, zw = z.shape
    gt = _tile(t, gt)
    rv = lax.broadcasted_iota(jnp.int32, (GLA_DV, GLA_DK), 0) // GLA_HDV
    ck = lax.broadcasted_iota(jnp.int32, (GLA_DV, GLA_DK), 1) // GLA_HDK
    bd = (rv == ck).astype(F32)
    return pl.pallas_call(
        _gla_prompt_kernel,
        grid=(t // gt,),
        in_specs=[pl.BlockSpec((nb, gt, zw), lambda i: (0, i, 0)),
                  pl.BlockSpec((GLA_GATE_RANK, GLA_DK), lambda i: (0, 0)),
                  pl.BlockSpec((1, GLA_DK), lambda i: (0, 0)),
                  pl.BlockSpec((1, GLA_DV), lambda i: (0, 0)),
                  pl.BlockSpec((GLA_DV, GLA_DK), lambda i: (0, 0))],
        out_specs=[pl.BlockSpec((nb, gt, GLA_DV), lambda i: (0, i, 0)),
                   pl.BlockSpec((nb, GLA_DV, GLA_DK), lambda i: (0, 0, 0))],
        out_shape=[jax.ShapeDtypeStruct((nb, t, GLA_DV), F32),
                   jax.ShapeDtypeStruct((nb, GLA_DV, GLA_DK), F32)],
        scratch_shapes=[pltpu.VMEM((nb, GLA_DV, GLA_DK), F32)],
        compiler_params=_cparams(("arbitrary",)),
        name="gla_prompt",
    )(z, w_gate_up.astype(BF16), b_gate.reshape(1, GLA_DK), jnp.tile(g_onorm, GLA_HEADS).reshape(1, GLA_DV), bd)


def _gla_gate_kernel(z_ref, wgu_ref, bg_ref, q_ref, ea_ref):
    z = z_ref[...]
    la = _gla_log_decay(z[:, GLA_LR_OFF2:GLA_LR_OFF2 + GLA_GATE_RANK], wgu_ref[...], bg_ref[...])
    ea_ref[...] = jnp.exp(la)
    q_ref[...] = z[:, 0:GLA_DK] * (GLA_HDK ** -0.5)


def _gla_sample_kernel(q_ref, k_ref, ea_ref, z_ref, gon_ref, s0_ref, tok_ref, s_ref):
    g = q_ref.shape[0]
    for j in range(g):
        zr = z_ref[j]
        v = zr[:, 2 * GLA_DK:GLA_R_OFF]
        r = zr[:, GLA_R_OFF:GLA_XQ_OFF2]
        v_rows = jnp.concatenate(
            [jnp.broadcast_to(v[:, h * GLA_HDV:(h + 1) * GLA_HDV], (GLA_HDK, GLA_HDV))
             for h in range(GLA_HEADS)], axis=0)
        s_new = ea_ref[j] * s0_ref[j] + k_ref[j] * v_rows
        s_ref[j] = s_new
        qs = q_ref[j] * s_new
        o = jnp.concatenate(
            [jnp.sum(qs[h * GLA_HDK:(h + 1) * GLA_HDK], axis=0, keepdims=True) for h in range(GLA_HEADS)],
            axis=1)
        tok_ref[j] = _gla_out(o, r, gon_ref[...])


def gla_sample(z, w_gate_up, b_gate, g_onorm, state0, g=8):
    bs, zw = z.shape
    g = _tile(bs, g)
    q, ea = pl.pallas_call(
        _gla_gate_kernel,
        out_shape=[jax.ShapeDtypeStruct((bs, GLA_DK), F32), jax.ShapeDtypeStruct((bs, GLA_DK), F32)],
        name="gla_gate",
    )(z, w_gate_up, b_gate.reshape(1, GLA_DK))
    k = z[:, GLA_DK:2 * GLA_DK]
    col = lambda a: a.reshape(bs, GLA_DK, 1)
    col_spec = pl.BlockSpec((g, GLA_DK, 1), lambda i: (i, 0, 0))
    tok, s_new = pl.pallas_call(
        _gla_sample_kernel,
        grid=(bs // g,),
        in_specs=[col_spec, col_spec, col_spec,
                  pl.BlockSpec((g, 1, zw), lambda i: (i, 0, 0)),
                  pl.BlockSpec((1, GLA_DV), lambda i: (0, 0)),
                  pl.BlockSpec((g, GLA_DK, GLA_HDV), lambda i: (i, 0, 0))],
        out_specs=[pl.BlockSpec((g, 1, GLA_DV), lambda i: (i, 0, 0)),
                   pl.BlockSpec((g, GLA_DK, GLA_HDV), lambda i: (i, 0, 0))],
        out_shape=[jax.ShapeDtypeStruct((bs, 1, GLA_DV), F32),
                   jax.ShapeDtypeStruct((bs, GLA_DK, GLA_HDV), F32)],
        compiler_params=_cparams(("parallel",)),
        name="gla_sample",
    )(col(q), col(k), col(ea), z.reshape(bs, 1, zw), jnp.tile(g_onorm, GLA_HEADS).reshape(1, GLA_DV), state0)
    return tok.reshape(bs, GLA_DV), s_new


def kernel(x_prompt, x_sample, cache_sb_k, cache_sb_v, state_gla, cache_mem_k, cache_mem_v, page_table, mem_prompt,
           g_mix, g_mem, w_mem_kv, w_in_sb, b_sb, w_in_gla, w_gate_up, b_gate, g_gla_onorm, w_out,
           g_ffn, w_ffn_gu, w_ffn_down, w_router, w_exp_gu, w_exp_down, g_final):
    B, T, D = x_prompt.shape
    Bs = x_sample.shape[0]
    n_mem = mem_prompt.shape[1]
    n_phys = cache_sb_k.shape[1]
    hp = x_prompt.reshape(B * T, D)
    hs = x_sample.reshape(Bs, D)
    mem = mem_prompt.reshape(B * n_mem, D)

    def mem_kv(i):
        mkv = norm_matmul(mem, g_mem[i], w_mem_kv[i].astype(BF16))
        mk = mkv[:, :X_W].reshape(B, n_mem, X_HEADS, HEAD_DIM)
        mv = mkv[:, X_W:].reshape(B, n_mem, X_HEADS, HEAD_DIM)
        mkt = jnp.transpose(mk, (0, 2, 3, 1)).astype(BF16)
        mvh = jnp.transpose(mv, (0, 2, 1, 3)).astype(BF16)
        return mk, mv, mkt, mvh

    def cross_s(xq, i):
        return cross_sample(xq, mem_k_t, mem_v_t, i)

    mem_k_t = jnp.transpose(cache_mem_k, (0, 1, 3, 4, 2))
    mem_v_t = jnp.transpose(cache_mem_v, (0, 1, 3, 4, 2))

    mk0, mv0, mkt, mvh = mem_kv(0)
    qt, kh, vt, k_p, v_p, xq_p = sb_inproj(hp, g_mix[0], w_in_sb[0].astype(BF16))
    tok_p = sb_prompt_attention(b_sb[0], qt, kh, vt, B)
    xa_p = cross_prompt(xq_p, mkt, mvh, B)
    hp = outproj(tok_p, xa_p, hp, w_out[0].astype(BF16))

    zs = norm_matmul(hs, g_mix[0], w_in_sb[0])
    k_s, v_s, xq_s = zs[:, TOK_W:2 * TOK_W], zs[:, 2 * TOK_W:3 * TOK_W], zs[:, 3 * TOK_W:]
    tok_s = sb_sample_attention(zs[:, :TOK_W], b_sb[0],
                                jnp.transpose(cache_sb_k[0], (0, 2, 3, 1)),
                                jnp.transpose(cache_sb_v[0], (0, 2, 3, 1)), page_table)
    hs = outproj(tok_s, cross_s(xq_s, 0), hs, w_out[0])

    hp = ffn(hp, g_ffn[0], w_ffn_gu[0].astype(BF16), w_ffn_down[0].astype(BF16))
    hs = ffn(hs, g_ffn[0], w_ffn_gu[0], w_ffn_down[0])

    w = w_in_gla[0]
    lr_off = GLA_R_OFF + GLA_DV
    w_in = jnp.concatenate([w[:, :lr_off], w[:, lr_off + GLA_GATE_RANK:], w[:, lr_off:lr_off + GLA_GATE_RANK]],
                           axis=1)
    mk1, mv1, mkt, mvh = mem_kv(1)
    zp = norm_matmul(hp, g_mix[1], w_in.astype(BF16))
    tok_p, st_p = gla_prompt(zp.reshape(B, T, -1), w_gate_up[0], b_gate[0], g_gla_onorm[0])
    xa_p = cross_prompt(zp[:, GLA_XQ_OFF2:GLA_XQ_OFF2 + X_W], mkt, mvh, B)
    hp = outproj(tok_p.reshape(B * T, GLA_DV), xa_p, hp, w_out[1].astype(BF16))

    zs = norm_matmul(hs, g_mix[1], w_in)
    tok_s, st_s = gla_sample(zs, w_gate_up[0], b_gate[0], g_gla_onorm[0],
                             state_gla[0].reshape(Bs, GLA_DK, GLA_HDV))
    hs = outproj(tok_s, cross_s(zs[:, GLA_XQ_OFF2:GLA_XQ_OFF2 + X_W], 1), hs, w_out[1])

    w_gu = w_exp_gu[0].astype(BF16)
    w_dn = w_exp_down[0].astype(BF16)
    y_p = moe_final(hp, g_ffn[1], w_router[0], w_gu, w_dn, g_final, tm=512)
    y_s = moe_final(hs, g_ffn[1], w_router[0], w_gu, w_dn, g_final, tm=128)

    st_p = jnp.stack([jnp.stack([st_p[b, h * GLA_HDV:(h + 1) * GLA_HDV, h * GLA_HDK:(h + 1) * GLA_HDK].T
                                 for h in range(GLA_HEADS)]) for b in range(B)])
    return (y_p.reshape(B, T, D), y_s.reshape(Bs, 1, D),
            k_p.reshape(1, B, T, SB_HEADS, HEAD_DIM), v_p.reshape(1, B, T, SB_HEADS, HEAD_DIM),
            k_s.reshape(1, Bs, 1, SB_HEADS, HEAD_DIM), v_s.reshape(1, Bs, 1, SB_HEADS, HEAD_DIM),
            st_p[None], st_s.reshape(1, Bs, GLA_HEADS, GLA_HDK, GLA_HDV),
            jnp.stack([mk0, mk1]), jnp.stack([mv0, mv1]))
```

```python
import functools

import jax
import jax.numpy as jnp
from jax import lax
from jax.experimental import pallas as pl
from jax.experimental.pallas import tpu as pltpu

F32 = jnp.float32
BF16 = jnp.bfloat16

D_MODEL = 1024
HEAD_DIM = 64
X_HEADS = 4
X_W = X_HEADS * HEAD_DIM
TOK_W = D_MODEL - X_W
SB_HEADS = TOK_W // HEAD_DIM
GLA_HEADS = 4
GLA_DV = TOK_W
GLA_DK = GLA_DV // 2
GLA_HDK = GLA_DK // GLA_HEADS
GLA_HDV = GLA_DV // GLA_HEADS
GLA_GATE_RANK = 16
GLA_TAU = 16.0
GLA_CHUNK = 64
GLA_SUB = 16
N_EXPERTS = 8
RMS_EPS = 1e-6
PAGE_SIZE = 128

LANES = 128
SB_KB = 256
VMEM_LIMIT = 56 * 1024 * 1024


def _cparams(sem):
    return pltpu.CompilerParams(dimension_semantics=sem, vmem_limit_bytes=VMEM_LIMIT)


def _rms(x, g):
    return x * lax.rsqrt(jnp.mean(x * x, axis=-1, keepdims=True) + RMS_EPS) * g


def _dot(a, b):
    return jnp.dot(a, b, preferred_element_type=F32)


def _split_bf16(x):
    hi = x.astype(BF16)
    lo = (x - hi.astype(F32)).astype(BF16)
    return hi, lo


def _dot_split(a, b):
    ah, al = _split_bf16(a)
    bh, bl = _split_bf16(b)
    return _dot(ah, bh) + (_dot(al, bh) + _dot(ah, bl))


def _matmul(a, w):
    if w.dtype == BF16:
        return _dot(a.astype(BF16), w)
    return _dot_split(a, w)


def _dot_nt(a, b):
    return lax.dot_general(a, b, (((1,), (1,)), ((), ())), preferred_element_type=F32)


def _dot_tn(a, b):
    return lax.dot_general(a, b, (((0,), (0,)), ((), ())), preferred_element_type=F32)


def _softplus(z):
    return jnp.maximum(z, 0.0) + jnp.log(1.0 + jnp.exp(-jnp.abs(z)))


def _tile(n, pref):
    t = min(n, pref)
    assert n % t == 0, (n, t)
    return t


def _norm_matmul_kernel(x_ref, g_ref, w_ref, o_ref):
    o_ref[...] = _matmul(_rms(x_ref[...], g_ref[...]), w_ref[...])


def norm_matmul(x, g, w_bf16, tm=512):
    n, d = x.shape
    m = w_bf16.shape[1]
    tm = _tile(n, tm)
    return pl.pallas_call(
        _norm_matmul_kernel,
        grid=(n // tm,),
        in_specs=[pl.BlockSpec((tm, d), lambda i: (i, 0)),
                  pl.BlockSpec((1, d), lambda i: (0, 0)),
                  pl.BlockSpec((d, m), lambda i: (0, 0))],
        out_specs=pl.BlockSpec((tm, m), lambda i: (i, 0)),
        out_shape=jax.ShapeDtypeStruct((n, m), F32),
        compiler_params=_cparams(("parallel",)),
        name="norm_matmul",
    )(x, g.reshape(1, d), w_bf16)


def _sb_inproj_kernel(x_ref, g_ref, w_ref, qt_ref, kh_ref, vt_ref, k_ref, v_ref, xq_ref):
    tm = x_ref.shape[0]
    xn = _rms(x_ref[...], g_ref[...]).astype(BF16)
    z = _dot(xn, w_ref[...])
    k_ref[...] = z[:, TOK_W:2 * TOK_W]
    v_ref[...] = z[:, 2 * TOK_W:3 * TOK_W]
    xq_ref[...] = z[:, 3 * TOK_W:]
    for h in range(SB_HEADS):
        lo = TOK_W + h * HEAD_DIM
        kh_ref[h] = z[:, lo:lo + HEAD_DIM].astype(BF16)
    for p in range(SB_HEADS // 2):
        qt2 = (z[:, p * LANES:(p + 1) * LANES] * (HEAD_DIM ** -0.5)).T
        vt2 = z[:, 2 * TOK_W + p * LANES:2 * TOK_W + (p + 1) * LANES].T
        for hh in range(2):
            rows = slice(hh * HEAD_DIM, (hh + 1) * HEAD_DIM)
            for c in range(tm // SB_KB):
                cols = slice(c * SB_KB, (c + 1) * SB_KB)
                qt_ref[2 * p + hh, c] = qt2[rows, cols].astype(BF16)
                vt_ref[2 * p + hh, c] = vt2[rows, cols].astype(BF16)


def sb_inproj(x, g, w_bf16, tm=512):
    n, d = x.shape
    m = w_bf16.shape[1]
    tm = _tile(n, tm)
    nkb = tm // SB_KB
    t_spec = pl.BlockSpec((SB_HEADS, nkb, HEAD_DIM, SB_KB), lambda i: (0, i, 0, 0))
    t_shape = jax.ShapeDtypeStruct((SB_HEADS, n // SB_KB, HEAD_DIM, SB_KB), BF16)
    return pl.pallas_call(
        _sb_inproj_kernel,
        grid=(n // tm,),
        in_specs=[pl.BlockSpec((tm, d), lambda i: (i, 0)),
                  pl.BlockSpec((1, d), lambda i: (0, 0)),
                  pl.BlockSpec((d, m), lambda i: (0, 0))],
        out_specs=[t_spec,
                   pl.BlockSpec((SB_HEADS, tm, HEAD_DIM), lambda i: (0, i, 0)),
                   t_spec,
                   pl.BlockSpec((tm, TOK_W), lambda i: (i, 0)),
                   pl.BlockSpec((tm, TOK_W), lambda i: (i, 0)),
                   pl.BlockSpec((tm, X_W), lambda i: (i, 0))],
        out_shape=[t_shape,
                   jax.ShapeDtypeStruct((SB_HEADS, n, HEAD_DIM), BF16),
                   t_shape,
                   jax.ShapeDtypeStruct((n, TOK_W), F32),
                   jax.ShapeDtypeStruct((n, TOK_W), F32),
                   jax.ShapeDtypeStruct((n, X_W), F32)],
        compiler_params=_cparams(("parallel",)),
        name="sb_inproj",
    )(x, g.reshape(1, d), w_bf16)


LOG2E = 1.4426950408889634


def _sb_prompt_kernel(bias_ref, qt_ref, k_ref, vt_ref, o_ref, z_sc, t_sc, sp_sc, a_sc):
    hp = pl.program_id(1)
    i = pl.program_id(2)
    tq = qt_ref.shape[-1]
    row = lax.broadcasted_iota(jnp.int32, (SB_KB, SB_KB), 0)
    col = lax.broadcasted_iota(jnp.int32, (SB_KB, SB_KB), 1)
    tri = (col > row).astype(BF16)
    causal = row < col
    heads = range(2)

    def scores(j, hh):
        k = k_ref[hh, pl.ds(pl.multiple_of(j * SB_KB, SB_KB), SB_KB), :]
        return _dot(k, qt_ref[hh, 0]) + bias_ref[2 * hp + hh]

    def stage1(z, hh, slot, mask):
        e = jnp.exp2(jnp.abs(z) * (-LOG2E))
        sp = jnp.maximum(z, 0.0) + jnp.log(1.0 + e)
        t = z - sp
        if mask is not None:
            sp = jnp.where(mask, sp, 0.0)
            t = jnp.where(mask, t, -jnp.inf)
        t_sc[slot, hh] = t
        sp_sc[hh] = sp.astype(BF16)
        return sp[0:8, :].astype(BF16).astype(F32)

    def tick(j1, slot, state, with_stage1):
        heads8, ws, carries, accs = state
        j3 = jnp.minimum(j1 + 2, i)
        pvs = [_dot(vt_ref[hh, j3], a_sc[hh]) for hh in heads]
        locs = [_dot(tri, sp_sc[hh]) for hh in heads]
        heads8_next = heads8
        if with_stage1:
            zs_next = [scores(jnp.maximum(j1 - 1, 0), hh) for hh in heads]
            heads8_next = [stage1(z_sc[slot, hh], hh, 1 - slot, None) for hh in heads]
            for hh in heads:
                z_sc[1 - slot, hh] = zs_next[hh]
        ws_next, new_c, new_a = [], [], []
        for hh in heads:
            a_sc[hh] = jnp.exp(t_sc[slot, hh] - locs[hh]).astype(BF16)
            ws_next.append(jnp.exp(-carries[hh]))
            total = locs[hh][0:1, :] + heads8[hh][0:1, :]
            new_c.append(carries[hh] + jnp.broadcast_to(total, carries[hh].shape))
            new_a.append(accs[hh] + ws[hh][0:1, :] * pvs[hh])
        return heads8_next, ws_next, new_c, new_a

    heads8 = [stage1(scores(i, hh), hh, 0, causal) for hh in heads]
    for hh in heads:
        z_sc[0, hh] = scores(jnp.maximum(i - 1, 0), hh)
        a_sc[hh] = jnp.zeros((SB_KB, tq), BF16)
    zrow = [jnp.minimum(h8, 0.0) for h8 in heads8]
    state = (heads8, zrow, zrow, [jnp.zeros((HEAD_DIM, tq), F32) for _ in heads])
    state = lax.fori_loop(0, i, lambda jj, st: tick(i - 1 - jj, jj & 1, st, True), state)
    _, ws, _, accs = tick(-1, i & 1, state, False)
    accs = [accs[hh] + ws[hh][0:1, :] * _dot(vt_ref[hh, 0], a_sc[hh]) for hh in heads]
    o_ref[...] = jnp.concatenate(accs, axis=0).T


def sb_prompt_attention(bias, qt, kh, vt, batch):
    n = kh.shape[1]
    t = n // batch
    tq = SB_KB
    nq = t // tq
    return pl.pallas_call(
        _sb_prompt_kernel,
        grid=(batch, SB_HEADS // 2, nq),
        in_specs=[pl.BlockSpec(memory_space=pltpu.SMEM),
                  pl.BlockSpec((2, 1, HEAD_DIM, tq), lambda b, h, i: (h, b * nq + i, 0, 0)),
                  pl.BlockSpec((2, t, HEAD_DIM), lambda b, h, i: (h, b, 0)),
                  pl.BlockSpec((2, nq, HEAD_DIM, SB_KB), lambda b, h, i: (h, b, 0, 0))],
        out_specs=pl.BlockSpec((tq, 2 * HEAD_DIM), lambda b, h, i: (b * nq + i, h)),
        out_shape=jax.ShapeDtypeStruct((n, TOK_W), F32),
        scratch_shapes=[pltpu.VMEM((2, 2, SB_KB, tq), F32), pltpu.VMEM((2, 2, SB_KB, tq), F32),
                        pltpu.VMEM((2, SB_KB, tq), BF16), pltpu.VMEM((2, SB_KB, tq), BF16)],
        compiler_params=_cparams(("parallel", "parallel", "arbitrary")),
        name="sb_prompt_attention",
    )(bias, qt, kh, vt)


SB_PAGES_PER_STEP = 16


def _split3_bf16(x):
    hi = x.astype(BF16)
    r = x - hi.astype(F32)
    mid = r.astype(BF16)
    lo = (r - mid.astype(F32)).astype(BF16)
    return hi, mid, lo


def _sb_sample_kernel(pt_ref, q_ref, bias_ref, *refs):
    del pt_ref
    npg = (len(refs) - 3) // 2
    k_refs, v_refs = refs[:npg], refs[npg:2 * npg]
    o_ref, carry_sc, acc_sc = refs[2 * npg:]
    g = pl.program_id(1)
    hpad = carry_sc.shape[0]

    @pl.when(g == 0)
    def _():
        carry_sc[...] = jnp.zeros_like(carry_sc)
        acc_sc[...] = jnp.zeros_like(acc_sc)

    r = lax.broadcasted_iota(jnp.int32, (PAGE_SIZE, PAGE_SIZE), 0)
    c = lax.broadcasted_iota(jnp.int32, (PAGE_SIZE, PAGE_SIZE), 1)
    tri = (r > c).astype(BF16)
    qs = [_row_to_col(q_ref[0, h:h + 1, :]) * (HEAD_DIM ** -0.5) for h in range(SB_HEADS)]
    pad = jnp.zeros((hpad - SB_HEADS, PAGE_SIZE), F32)
    carry = carry_sc[...]
    for p in range(npg):
        rows = [jnp.sum(k_refs[p][0, h] * qs[h], axis=0, keepdims=True) for h in range(SB_HEADS)]
        z = jnp.concatenate(rows + [pad], axis=0) + bias_ref[...]
        sp = _softplus(z)
        hi, mid, lo = _split3_bf16(sp)
        loc = _dot(hi, tri) + (_dot(mid, tri) + _dot(lo, tri))
        a = jnp.exp(z - sp - loc - carry)
        for h in range(SB_HEADS):
            acc_sc[h] += a[h:h + 1, :] * v_refs[p][0, h]
        carry = carry + jnp.broadcast_to(loc[:, 0:1] + sp[:, 0:1], carry.shape)
    carry_sc[...] = carry

    @pl.when(g == pl.num_programs(1) - 1)
    def _():
        o_ref[0] = jnp.concatenate(
            [_col_to_row(jnp.sum(acc_sc[h], axis=-1, keepdims=True)) for h in range(SB_HEADS)], axis=0)


def sb_sample_attention(q, bias, cache_k, cache_v, page_table):
    bs = q.shape[0]
    n_pages = page_table.shape[1]
    npg = _tile(n_pages, SB_PAGES_PER_STEP)
    hpad = 16
    bias_b = jnp.zeros((hpad, PAGE_SIZE), F32).at[:SB_HEADS].set(
        jnp.broadcast_to(bias[:, None], (SB_HEADS, PAGE_SIZE)))
    pt = page_table.reshape(-1)

    def page_spec(p):
        def index_map(b, g, pt_ref):
            return (pt_ref[b * n_pages + (n_pages - 1 - g * npg - p)], 0, 0, 0)
        return pl.BlockSpec((1, SB_HEADS, HEAD_DIM, PAGE_SIZE), index_map)

    qo_spec = pl.BlockSpec((1, SB_HEADS, HEAD_DIM), lambda b, g, pt_ref: (b, 0, 0))
    grid_spec = pltpu.PrefetchScalarGridSpec(
        num_scalar_prefetch=1,
        grid=(bs, n_pages // npg),
        in_specs=[qo_spec, pl.BlockSpec((hpad, PAGE_SIZE), lambda b, g, pt_ref: (0, 0))]
        + [page_spec(p) for p in range(npg)] * 2,
        out_specs=qo_spec,
        scratch_shapes=[pltpu.VMEM((hpad, PAGE_SIZE), F32), pltpu.VMEM((SB_HEADS, HEAD_DIM, PAGE_SIZE), F32)],
    )
    out = pl.pallas_call(
        _sb_sample_kernel,
        grid_spec=grid_spec,
        out_shape=jax.ShapeDtypeStruct((bs, SB_HEADS, HEAD_DIM), F32),
        compiler_params=_cparams(("parallel", "arbitrary")),
        name="sb_sample_attention",
    )(pt, q.reshape(bs, SB_HEADS, HEAD_DIM), bias_b, *([cache_k] * npg), *([cache_v] * npg))
    return out.reshape(bs, TOK_W)


def _softmax_rows(s):
    m = jnp.max(s, axis=-1, keepdims=True)
    e = jnp.exp(s - m)
    return e / jnp.sum(e, axis=-1, keepdims=True)


def _cross_prompt_kernel(q_ref, mkt_ref, mv_ref, o_ref):
    q = q_ref[...]
    for h in range(X_HEADS):
        lo = h * HEAD_DIM
        qh = q[:, lo:lo + HEAD_DIM].astype(BF16)
        s = _dot(qh, mkt_ref[0, h]) * (HEAD_DIM ** -0.5)
        p = _softmax_rows(s).astype(BF16)
        o_ref[:, lo:lo + HEAD_DIM] = _dot(p, mv_ref[0, h])


def cross_prompt(xq, mkt, mvh, batch, tq=512):
    n = xq.shape[0]
    t = n // batch
    tq = _tile(t, tq)
    nq = t // tq
    nm = mkt.shape[-1]
    return pl.pallas_call(
        _cross_prompt_kernel,
        grid=(batch, nq),
        in_specs=[pl.BlockSpec((tq, X_W), lambda b, i: (b * nq + i, 0)),
                  pl.BlockSpec((1, X_HEADS, HEAD_DIM, nm), lambda b, i: (b, 0, 0, 0)),
                  pl.BlockSpec((1, X_HEADS, nm, HEAD_DIM), lambda b, i: (b, 0, 0, 0))],
        out_specs=pl.BlockSpec((tq, X_W), lambda b, i: (b * nq + i, 0)),
        out_shape=jax.ShapeDtypeStruct((n, X_W), F32),
        compiler_params=_cparams(("parallel", "parallel")),
        name="cross_prompt",
    )(xq, mkt, mvh)


def _eye(n):
    return lax.broadcasted_iota(jnp.int32, (n, n), 0) == lax.broadcasted_iota(jnp.int32, (n, n), 1)


def _row_to_col(row):
    n = row.shape[1]
    return jnp.sum(jnp.where(_eye(n), jnp.broadcast_to(row, (n, n)), 0.0), axis=1, keepdims=True)


def _col_to_row(col):
    n = col.shape[0]
    return jnp.sum(jnp.where(_eye(n), jnp.broadcast_to(col, (n, n)), 0.0), axis=0, keepdims=True)


def _cross_sample_kernel(q_ref, mk_ref, mv_ref, o_ref):
    for j in range(q_ref.shape[0]):
        rows = []
        for h in range(X_HEADS):
            q = _row_to_col(q_ref[j, h:h + 1, :])
            s = jnp.sum(mk_ref[0, j, h] * q, axis=0, keepdims=True) * (HEAD_DIM ** -0.5)
            p = _softmax_rows(s)
            rows.append(_col_to_row(jnp.sum(mv_ref[0, j, h] * p, axis=-1, keepdims=True)))
        o_ref[j] = jnp.concatenate(rows, axis=0)


def cross_sample(xq, mk, mv, layer, g=8):
    bs = xq.shape[0]
    g = _tile(bs, g)
    nm = mk.shape[-1]
    qo_spec = pl.BlockSpec((g, X_HEADS, HEAD_DIM), lambda i: (i, 0, 0))
    m_spec = pl.BlockSpec((1, g, X_HEADS, HEAD_DIM, nm), lambda i: (layer, i, 0, 0, 0))
    out = pl.pallas_call(
        _cross_sample_kernel,
        grid=(bs // g,),
        in_specs=[qo_spec, m_spec, m_spec],
        out_specs=qo_spec,
        out_shape=jax.ShapeDtypeStruct((bs, X_HEADS, HEAD_DIM), F32),
        compiler_params=_cparams(("parallel",)),
        name="cross_sample",
    )(xq.reshape(bs, X_HEADS, HEAD_DIM), mk, mv)
    return out.reshape(bs, X_W)


def _outproj_kernel(tok_ref, xa_ref, h_ref, wt_ref, wb_ref, o_ref):
    o_ref[...] = h_ref[...] + _matmul(tok_ref[...], wt_ref[...]) + _matmul(xa_ref[...], wb_ref[...])


def outproj(tok, xa, h, w_bf16, tm=512):
    n, d = h.shape
    tm = _tile(n, tm)
    wt, wb = w_bf16[:TOK_W], w_bf16[TOK_W:]
    return pl.pallas_call(
        _outproj_kernel,
        grid=(n // tm,),
        in_specs=[pl.BlockSpec((tm, TOK_W), lambda i: (i, 0)),
                  pl.BlockSpec((tm, X_W), lambda i: (i, 0)),
                  pl.BlockSpec((tm, d), lambda i: (i, 0)),
                  pl.BlockSpec((TOK_W, d), lambda i: (0, 0)),
                  pl.BlockSpec((X_W, d), lambda i: (0, 0))],
        out_specs=pl.BlockSpec((tm, d), lambda i: (i, 0)),
        out_shape=jax.ShapeDtypeStruct((n, d), F32),
        compiler_params=_cparams(("parallel",)),
        name="outproj",
    )(tok, xa, h, wt, wb)


def _ffn_kernel(x_ref, g_ref, wg_ref, wu_ref, wd_ref, o_ref, xn_sc, acc_sc):
    f = pl.program_id(1)

    @pl.when(f == 0)
    def _():
        xn_sc[...] = _rms(x_ref[...], g_ref[...]).astype(xn_sc.dtype)
        acc_sc[...] = jnp.zeros_like(acc_sc)

    xn = xn_sc[...]
    gate = _matmul(xn, wg_ref[...])
    up = _matmul(xn, wu_ref[...])
    acc_sc[...] += _matmul(jax.nn.silu(gate) * up, wd_ref[...])

    @pl.when(f == pl.num_programs(1) - 1)
    def _():
        o_ref[...] = x_ref[...] + acc_sc[...]


def ffn(x, g, w_gu, w_down, tm=1024, tf=256):
    n, d = x.shape
    dff = w_down.shape[0]
    tm = _tile(n, tm)
    nf = dff // tf
    return pl.pallas_call(
        _ffn_kernel,
        grid=(n // tm, nf),
        in_specs=[pl.BlockSpec((tm, d), lambda i, f: (i, 0)),
                  pl.BlockSpec((1, d), lambda i, f: (0, 0)),
                  pl.BlockSpec((d, tf), lambda i, f: (0, f)),
                  pl.BlockSpec((d, tf), lambda i, f: (0, f + nf)),
                  pl.BlockSpec((tf, d), lambda i, f: (f, 0))],
        out_specs=pl.BlockSpec((tm, d), lambda i, f: (i, 0)),
        out_shape=jax.ShapeDtypeStruct((n, d), F32),
        scratch_shapes=[pltpu.VMEM((tm, d), w_gu.dtype), pltpu.VMEM((tm, d), F32)],
        compiler_params=_cparams(("parallel", "arbitrary")),
        name="ffn",
    )(x, g.reshape(1, d), w_gu, w_gu, w_down)


def _router_kernel(x_ref, g_ref, w_ref, o_ref, cnt_ref, cnt_sc):
    @pl.when(pl.program_id(0) == 0)
    def _():
        cnt_sc[...] = jnp.zeros_like(cnt_sc)

    xn = _rms(x_ref[...], g_ref[...])
    logits = _dot_split(xn, w_ref[...])
    lane = lax.broadcasted_iota(jnp.int32, logits.shape, 1)
    neg = jnp.float32(-jnp.inf)
    logits = jnp.where(lane < N_EXPERTS, logits, neg)
    m1 = jnp.max(logits, axis=-1, keepdims=True)
    i1 = jnp.min(jnp.where(logits == m1, lane, LANES), axis=-1, keepdims=True)
    rest = jnp.where(lane == i1, neg, logits)
    m2 = jnp.max(rest, axis=-1, keepdims=True)
    i2 = jnp.min(jnp.where(rest == m2, lane, LANES), axis=-1, keepdims=True)
    e2 = jnp.exp(m2 - m1)
    den = 1.0 + e2
    oh1 = lane == i1
    oh2 = lane == i2
    cnt = (oh1 | oh2).astype(BF16)
    tm = cnt.shape[0]
    r = lax.broadcasted_iota(jnp.int32, (tm, tm), 0)
    c = lax.broadcasted_iota(jnp.int32, (tm, tm), 1)
    before = _dot((c < r).astype(BF16), cnt) + cnt_sc[...]
    r1 = jnp.sum(jnp.where(oh1, before, 0.0), axis=-1, keepdims=True)
    r2 = jnp.sum(jnp.where(oh2, before, 0.0), axis=-1, keepdims=True)
    cnt_sc[...] = before[tm - 1:tm, :] + cnt[tm - 1:tm, :].astype(F32)
    cnt_ref[...] = cnt_sc[...]
    info = jnp.zeros(logits.shape, F32)
    for k, val in enumerate((i1.astype(F32), i2.astype(F32), 1.0 / den, e2 / den, r1, r2)):
        info = jnp.where(lane == k, val, info)
    o_ref[...] = info


R_E1, R_E2, R_G1, R_G2, R_R1, R_R2 = range(6)


def router(x, g, w_router, tm=512):
    n, d = x.shape
    tm = _tile(n, tm)
    w_pad = jnp.zeros((d, LANES), F32).at[:, :N_EXPERTS].set(w_router)
    return pl.pallas_call(
        _router_kernel,
        grid=(n // tm,),
        in_specs=[pl.BlockSpec((tm, d), lambda i: (i, 0)),
                  pl.BlockSpec((1, d), lambda i: (0, 0)),
                  pl.BlockSpec((d, LANES), lambda i: (0, 0))],
        out_specs=[pl.BlockSpec((tm, LANES), lambda i: (i, 0)),
                   pl.BlockSpec((1, LANES), lambda i: (0, 0))],
        out_shape=[jax.ShapeDtypeStruct((n, LANES), F32), jax.ShapeDtypeStruct((1, LANES), F32)],
        scratch_shapes=[pltpu.VMEM((1, LANES), F32)],
        compiler_params=_cparams(("arbitrary",)),
        name="router",
    )(x, g.reshape(1, d), w_pad)


def _slot_sources_kernel(s1_ref, s2_ref, src_ref):
    def clear(s, carry):
        src_ref[s] = 0
        return carry

    def put(t, carry):
        src_ref[s1_ref[t]] = t
        src_ref[s2_ref[t]] = t
        return carry

    lax.fori_loop(0, src_ref.shape[0], clear, 0, unroll=8)
    lax.fori_loop(0, s1_ref.shape[0], put, 0, unroll=8)


def slot_sources(slot1, slot2, n_slots):
    return pl.pallas_call(
        _slot_sources_kernel,
        in_specs=[pl.BlockSpec(memory_space=pltpu.SMEM), pl.BlockSpec(memory_space=pltpu.SMEM)],
        out_specs=pl.BlockSpec(memory_space=pltpu.SMEM),
        out_shape=jax.ShapeDtypeStruct((n_slots,), jnp.int32),
        name="slot_sources",
    )(slot1, slot2)


def _row_copy(src_hbm, src_row, dst_ref, dst_row, sem):
    return pltpu.make_async_copy(src_hbm.at[pl.ds(src_row, 1)], dst_ref.at[pl.ds(dst_row, 1)], sem)


def _expert_kernel(te_ref, nv_ref, src_ref, x_hbm, g_ref, wg_ref, wu_ref, wd_ref, o_ref, xbuf, xn_sc, acc_sc, sem):
    del te_ref
    t = pl.program_id(0)
    f = pl.program_id(1)
    nf = pl.num_programs(1)
    tm = xbuf.shape[1]
    n_valid = nv_ref[0]
    half = t % 2

    def fetch(tile, first_row, n_rows, dst_half):
        for j in range(n_rows):
            r = first_row + j
            _row_copy(x_hbm, src_ref[tile * tm + r], xbuf.at[dst_half], r, sem.at[dst_half]).start()

    @pl.when((t == 0) & (f == 0))
    def _():
        fetch(0, 0, tm, 0)

    @pl.when((f == 0) & (t <= n_valid))
    def _():
        def wait(r, carry):
            _row_copy(x_hbm, 0, xbuf.at[half], r, sem.at[half]).wait()
            return carry
        lax.fori_loop(0, tm, wait, 0, unroll=8)

    @pl.when(t < n_valid)
    def _():
        @pl.when(f == 0)
        def _():
            xn_sc[...] = _rms(xbuf[half], g_ref[...]).astype(BF16)
            acc_sc[...] = jnp.zeros_like(acc_sc)

        xn = xn_sc[...]
        gate = _dot(xn, wg_ref[0])
        up = _dot(xn, wu_ref[0])
        act = (jax.nn.silu(gate) * up).astype(BF16)
        part = tm // nf
        fetch(t + 1, f * part, part, 1 - half)
        acc_sc[...] += _dot(act, wd_ref[0])

        @pl.when(f == nf - 1)
        def _():
            o_ref[...] = acc_sc[...]

    @pl.when((t >= n_valid) & (f == nf - 1))
    def _():
        o_ref[...] = jnp.zeros_like(o_ref)


def expert_swiglu(x, g, src, tile_expert, n_valid, w_gu_bf16, w_down_bf16, tm, tf=896):
    n_slots = src.shape[0]
    d = x.shape[1]
    dff = w_down_bf16.shape[1]
    nf = dff // tf
    assert tm % nf == 0

    def wmap(off):
        def index_map(t, f, te_ref, nv_ref, src_ref):
            return (te_ref[t], 0, jnp.where(t < nv_ref[0], f, nf - 1) + off)
        return index_map

    def dmap(t, f, te_ref, nv_ref, src_ref):
        return (te_ref[t], jnp.where(t < nv_ref[0], f, nf - 1), 0)

    grid_spec = pltpu.PrefetchScalarGridSpec(
        num_scalar_prefetch=3,
        grid=(n_slots // tm, nf),
        in_specs=[pl.BlockSpec(memory_space=pl.ANY),
                  pl.BlockSpec((1, d), lambda t, f, te_ref, nv_ref, src_ref: (0, 0)),
                  pl.BlockSpec((1, d, tf), wmap(0)),
                  pl.BlockSpec((1, d, tf), wmap(nf)),
                  pl.BlockSpec((1, tf, d), dmap)],
        out_specs=pl.BlockSpec((tm, d), lambda t, f, te_ref, nv_ref, src_ref: (t, 0)),
        scratch_shapes=[pltpu.VMEM((2, tm, d), F32), pltpu.VMEM((tm, d), BF16), pltpu.VMEM((tm, d), F32),
                        pltpu.SemaphoreType.DMA((2,))],
    )
    return pl.pallas_call(
        _expert_kernel,
        grid_spec=grid_spec,
        out_shape=jax.ShapeDtypeStruct((n_slots, d), F32),
        compiler_params=_cparams(("arbitrary", "arbitrary")),
        name="expert_swiglu",
    )(tile_expert, n_valid, src, x, g.reshape(1, d), w_gu_bf16, w_gu_bf16, w_down_bf16)


def _combine_kernel(s1_ref, s2_ref, x_ref, info_ref, gf_ref, ys_hbm, o_ref, buf, sem):
    tc = x_ref.shape[0]
    base = pl.program_id(0) * tc

    def start(r, carry):
        _row_copy(ys_hbm, s1_ref[base + r], buf.at[0], r, sem).start()
        _row_copy(ys_hbm, s2_ref[base + r], buf.at[1], r, sem).start()
        return carry

    def wait(r, carry):
        _row_copy(ys_hbm, 0, buf.at[0], r, sem).wait()
        _row_copy(ys_hbm, 0, buf.at[1], r, sem).wait()
        return carry

    lax.fori_loop(0, tc, start, 0, unroll=8)
    lax.fori_loop(0, tc, wait, 0, unroll=8)
    info = info_ref[...]
    g1 = info[:, R_G1:R_G1 + 1]
    g2 = info[:, R_G2:R_G2 + 1]
    o_ref[...] = _rms(x_ref[...] + (g1 * buf[0] + g2 * buf[1]), gf_ref[...])


def combine_final(x, info, slot1, slot2, ys, g_final, tc=256):
    n, d = x.shape
    tc = _tile(n, tc)
    grid_spec = pltpu.PrefetchScalarGridSpec(
        num_scalar_prefetch=2,
        grid=(n // tc,),
        in_specs=[pl.BlockSpec((tc, d), lambda i, s1, s2: (i, 0)),
                  pl.BlockSpec((tc, LANES), lambda i, s1, s2: (i, 0)),
                  pl.BlockSpec((1, d), lambda i, s1, s2: (0, 0)),
                  pl.BlockSpec(memory_space=pl.ANY)],
        out_specs=pl.BlockSpec((tc, d), lambda i, s1, s2: (i, 0)),
        scratch_shapes=[pltpu.VMEM((2, tc, d), F32), pltpu.SemaphoreType.DMA(())],
    )
    return pl.pallas_call(
        _combine_kernel,
        grid_spec=grid_spec,
        out_shape=jax.ShapeDtypeStruct((n, d), F32),
        compiler_params=_cparams(("arbitrary",)),
        name="combine_final",
    )(slot1, slot2, x, info, g_final.reshape(1, d), ys)


def moe_final(x, g, w_router, w_gu_bf16, w_down_bf16, g_final, tm):
    n, d = x.shape
    info, counts = router(x, g, w_router)
    counts = counts[0, :N_EXPERTS].astype(jnp.int32)
    padded = ((counts + tm - 1) // tm) * tm
    ends = jnp.cumsum(padded)
    offs = ends - padded
    n_slots = (-(-2 * n // tm) + N_EXPERTS) * tm
    e1 = info[:, R_E1].astype(jnp.int32)
    e2 = info[:, R_E2].astype(jnp.int32)
    slot1 = offs[e1] + info[:, R_R1].astype(jnp.int32)
    slot2 = offs[e2] + info[:, R_R2].astype(jnp.int32)
    src = slot_sources(slot1, slot2, n_slots)
    tile_start = jnp.arange(n_slots // tm, dtype=jnp.int32) * tm
    n_valid = (ends[-1] // tm).reshape(1)
    tile_expert = jnp.minimum(jnp.sum(tile_start[:, None] >= ends[None, :], axis=1), N_EXPERTS - 1).astype(jnp.int32)
    tile_expert = jnp.where(tile_start < ends[-1], tile_expert, tile_expert[jnp.maximum(n_valid[0] - 1, 0)])
    ys = expert_swiglu(x, g, src, tile_expert, n_valid, w_gu_bf16, w_down_bf16, tm)
    return combine_final(x, info, slot1, slot2, ys, g_final)


GLA_R_OFF = 2 * GLA_DK + GLA_DV
GLA_XQ_OFF2 = GLA_R_OFF + GLA_DV
GLA_LR_OFF2 = GLA_XQ_OFF2 + X_W


def _head_of(col, width):
    return ((col >= width).astype(jnp.int32) + (col >= 2 * width).astype(jnp.int32)
            + (col >= 3 * width).astype(jnp.int32))


def _gla_log_decay(lr, wgu, bg):
    gl = _matmul(lr, wgu) + bg
    return -_softplus(-gl) / GLA_TAU


def _gla_out(o, r, gon):
    col = lax.broadcasted_iota(jnp.int32, (1, GLA_DV), 1)
    hv = _head_of(col, GLA_HDV)
    o2 = o * o
    inv = jnp.zeros_like(o)
    for h in range(GLA_HEADS):
        m = hv == h
        ms = jnp.sum(jnp.where(m, o2, 0.0), axis=-1, keepdims=True) * (1.0 / GLA_HDV)
        inv = jnp.where(m, lax.rsqrt(ms + RMS_EPS), inv)
    return (o * inv * gon) * jax.nn.silu(r)


def _gla_prompt_kernel(z_ref, wgu_ref, bg_ref, gon_ref, bd_ref, tok_ref, st_ref, st_sc):
    nb = z_ref.shape[0]
    gt = z_ref.shape[1]
    c = GLA_CHUNK

    @pl.when(pl.program_id(0) == 0)
    def _():
        st_sc[...] = jnp.zeros_like(st_sc)

    rr = lax.broadcasted_iota(jnp.int32, (c, c), 0)
    cc = lax.broadcasted_iota(jnp.int32, (c, c), 1)
    ltri = (cc <= rr).astype(BF16)
    r4 = lax.broadcasted_iota(jnp.int32, (c, GLA_HEADS * c), 0)
    c4 = lax.broadcasted_iota(jnp.int32, (c, GLA_HEADS * c), 1)
    intra_mask = (c4 & (c - 1)) <= r4
    key_row = lax.broadcasted_iota(jnp.int32, (c, 1), 0)
    hk = _head_of(lax.broadcasted_iota(jnp.int32, (1, GLA_DK), 1), GLA_HDK)
    hv = _head_of(lax.broadcasted_iota(jnp.int32, (1, GLA_DV), 1), GLA_HDV)
    wgu = wgu_ref[...]
    bg = bg_ref[...]
    gon = gon_ref[...]
    bd = bd_ref[...]

    for ci in range(gt // c):
        for b in range(nb):
            zc = z_ref[b, ci * c:(ci + 1) * c, :]
            q = zc[:, 0:GLA_DK] * (GLA_HDK ** -0.5)
            k = zc[:, GLA_DK:2 * GLA_DK]
            v = zc[:, 2 * GLA_DK:GLA_R_OFF]
            r = zc[:, GLA_R_OFF:GLA_XQ_OFF2]
            lr = zc[:, GLA_LR_OFF2:GLA_LR_OFF2 + GLA_GATE_RANK]
            la = _gla_log_decay(lr, wgu, bg)
            hi = la.astype(BF16)
            r1 = la - hi.astype(F32)
            mid = r1.astype(BF16)
            low = (r1 - mid.astype(F32)).astype(BF16)
            bc = _dot(ltri, hi) + (_dot(ltri, mid) + _dot(ltri, low))
            qe = (q * jnp.exp(bc)).astype(BF16)
            vb = v.astype(BF16)
            v_stack = jnp.concatenate(
                [jnp.where(hv == h, vb, jnp.zeros_like(vb)) for h in range(GLA_HEADS)], axis=0)
            parts = []
            for g0 in range(0, c, GLA_SUB):
                ref = bc[g0 - 1:g0, :] if g0 else jnp.zeros((1, GLA_DK), F32)
                qg = (q[g0:g0 + GLA_SUB] * jnp.exp(bc[g0:g0 + GLA_SUB] - ref)).astype(BF16)
                kg = (k * jnp.exp(jnp.where(key_row < g0 + GLA_SUB, ref - bc, -jnp.inf))).astype(BF16)
                kg_stack = jnp.concatenate(
                    [jnp.where(hk == h, kg, jnp.zeros_like(kg)) for h in range(GLA_HEADS)], axis=0)
                parts.append(_dot_nt(qg, kg_stack))
            scores = jnp.where(intra_mask, jnp.concatenate(parts, axis=0), 0.0)
            o_intra = _dot(scores.astype(BF16), v_stack)
            st = st_sc[b]
            o_inter = _dot_nt(qe, st.astype(BF16))
            tok_ref[b, ci * c:(ci + 1) * c, :] = _gla_out(o_inter + o_intra, r, gon)
            b_last = bc[c - 1:c, :]
            kd = (k * jnp.exp(b_last - bc)).astype(BF16)
            st_sc[b] = st * jnp.exp(b_last) + _dot_tn(vb, kd) * bd

    @pl.when(pl.program_id(0) == pl.num_programs(0) - 1)
    def _():
        st_ref[...] = st_sc[...]


def gla_prompt(z, w_gate_up, b_gate, g_onorm, gt=256):
    nb, t, zw = z.shape
    gt = _tile(t, gt)
    rv = lax.broadcasted_iota(jnp.int32, (GLA_DV, GLA_DK), 0) // GLA_HDV
    ck = lax.broadcasted_iota(jnp.int32, (GLA_DV, GLA_DK), 1) // GLA_HDK
    bd = (rv == ck).astype(F32)
    return pl.pallas_call(
        _gla_prompt_kernel,
        grid=(t // gt,),
        in_specs=[pl.BlockSpec((nb, gt, zw), lambda i: (0, i, 0)),
                  pl.BlockSpec((GLA_GATE_RANK, GLA_DK), lambda i: (0, 0)),
                  pl.BlockSpec((1, GLA_DK), lambda i: (0, 0)),
                  pl.BlockSpec((1, GLA_DV), lambda i: (0, 0)),
                  pl.BlockSpec((GLA_DV, GLA_DK), lambda i: (0, 0))],
        out_specs=[pl.BlockSpec((nb, gt, GLA_DV), lambda i: (0, i, 0)),
                   pl.BlockSpec((nb, GLA_DV, GLA_DK), lambda i: (0, 0, 0))],
        out_shape=[jax.ShapeDtypeStruct((nb, t, GLA_DV), F32),
                   jax.ShapeDtypeStruct((nb, GLA_DV, GLA_DK), F32)],
        scratch_shapes=[pltpu.VMEM((nb, GLA_DV, GLA_DK), F32)],
        compiler_params=_cparams(("arbitrary",)),
        name="gla_prompt",
    )(z, w_gate_up.astype(BF16), b_gate.reshape(1, GLA_DK), jnp.tile(g_onorm, GLA_HEADS).reshape(1, GLA_DV), bd)


def _gla_gate_kernel(z_ref, wgu_ref, bg_ref, q_ref, ea_ref):
    z = z_ref[...]
    la = _gla_log_decay(z[:, GLA_LR_OFF2:GLA_LR_OFF2 + GLA_GATE_RANK], wgu_ref[...], bg_ref[...])
    ea_ref[...] = jnp.exp(la)
    q_ref[...] = z[:, 0:GLA_DK] * (GLA_HDK ** -0.5)


def _gla_sample_kernel(q_ref, k_ref, ea_ref, z_ref, gon_ref, s0_ref, tok_ref, s_ref):
    g = q_ref.shape[0]
    for j in range(g):
        zr = z_ref[j]
        v = zr[:, 2 * GLA_DK:GLA_R_OFF]
        r = zr[:, GLA_R_OFF:GLA_XQ_OFF2]
        v_rows = jnp.concatenate(
            [jnp.broadcast_to(v[:, h * GLA_HDV:(h + 1) * GLA_HDV], (GLA_HDK, GLA_HDV))
             for h in range(GLA_HEADS)], axis=0)
        s_new = ea_ref[j] * s0_ref[j] + k_ref[j] * v_rows
        s_ref[j] = s_new
        qs = q_ref[j] * s_new
        o = jnp.concatenate(
            [jnp.sum(qs[h * GLA_HDK:(h + 1) * GLA_HDK], axis=0, keepdims=True) for h in range(GLA_HEADS)],
            axis=1)
        tok_ref[j] = _gla_out(o, r, gon_ref[...])


def gla_sample(z, w_gate_up, b_gate, g_onorm, state0, g=8):
    bs, zw = z.shape
    g = _tile(bs, g)
    q, ea = pl.pallas_call(
        _gla_gate_kernel,
        out_shape=[jax.ShapeDtypeStruct((bs, GLA_DK), F32), jax.ShapeDtypeStruct((bs, GLA_DK), F32)],
        name="gla_gate",
    )(z, w_gate_up, b_gate.reshape(1, GLA_DK))
    k = z[:, GLA_DK:2 * GLA_DK]
    col = lambda a: a.reshape(bs, GLA_DK, 1)
    col_spec = pl.BlockSpec((g, GLA_DK, 1), lambda i: (i, 0, 0))
    tok, s_new = pl.pallas_call(
        _gla_sample_kernel,
        grid=(bs // g,),
        in_specs=[col_spec, col_spec, col_spec,
                  pl.BlockSpec((g, 1, zw), lambda i: (i, 0, 0)),
                  pl.BlockSpec((1, GLA_DV), lambda i: (0, 0)),
                  pl.BlockSpec((g, GLA_DK, GLA_HDV), lambda i: (i, 0, 0))],
        out_specs=[pl.BlockSpec((g, 1, GLA_DV), lambda i: (i, 0, 0)),
                   pl.BlockSpec((g, GLA_DK, GLA_HDV), lambda i: (i, 0, 0))],
        out_shape=[jax.ShapeDtypeStruct((bs, 1, GLA_DV), F32),
                   jax.ShapeDtypeStruct((bs, GLA_DK, GLA_HDV), F32)],
        compiler_params=_cparams(("parallel",)),
        name="gla_sample",
    )(col(q), col(k), col(ea), z.reshape(bs, 1, zw), jnp.tile(g_onorm, GLA_HEADS).reshape(1, GLA_DV), state0)
    return tok.reshape(bs, GLA_DV), s_new


def kernel(x_prompt, x_sample, cache_sb_k, cache_sb_v, state_gla, cache_mem_k, cache_mem_v, page_table, mem_prompt,
           g_mix, g_mem, w_mem_kv, w_in_sb, b_sb, w_in_gla, w_gate_up, b_gate, g_gla_onorm, w_out,
           g_ffn, w_ffn_gu, w_ffn_down, w_router, w_exp_gu, w_exp_down, g_final):
    B, T, D = x_prompt.shape
    Bs = x_sample.shape[0]
    n_mem = mem_prompt.shape[1]
    n_phys = cache_sb_k.shape[1]
    hp = x_prompt.reshape(B * T, D)
    hs = x_sample.reshape(Bs, D)
    mem = mem_prompt.reshape(B * n_mem, D)

    def mem_kv(i):
        mkv = norm_matmul(mem, g_mem[i], w_mem_kv[i].astype(BF16))
        mk = mkv[:, :X_W].reshape(B, n_mem, X_HEADS, HEAD_DIM)
        mv = mkv[:, X_W:].reshape(B, n_mem, X_HEADS, HEAD_DIM)
        mkt = jnp.transpose(mk, (0, 2, 3, 1)).astype(BF16)
        mvh = jnp.transpose(mv, (0, 2, 1, 3)).astype(BF16)
        return mk, mv, mkt, mvh

    def cross_s(xq, i):
        return cross_sample(xq, mem_k_t, mem_v_t, i)

    mem_k_t = jnp.transpose(cache_mem_k, (0, 1, 3, 4, 2))
    mem_v_t = jnp.transpose(cache_mem_v, (0, 1, 3, 4, 2))

    mk0, mv0, mkt, mvh = mem_kv(0)
    qt, kh, vt, k_p, v_p, xq_p = sb_inproj(hp, g_mix[0], w_in_sb[0].astype(BF16))
    tok_p = sb_prompt_attention(b_sb[0], qt, kh, vt, B)
    xa_p = cross_prompt(xq_p, mkt, mvh, B)
    hp = outproj(tok_p, xa_p, hp, w_out[0].astype(BF16))

    zs = norm_matmul(hs, g_mix[0], w_in_sb[0])
    k_s, v_s, xq_s = zs[:, TOK_W:2 * TOK_W], zs[:, 2 * TOK_W:3 * TOK_W], zs[:, 3 * TOK_W:]
    tok_s = sb_sample_attention(zs[:, :TOK_W], b_sb[0],
                                jnp.transpose(cache_sb_k[0], (0, 2, 3, 1)),
                                jnp.transpose(cache_sb_v[0], (0, 2, 3, 1)), page_table)
    hs = outproj(tok_s, cross_s(xq_s, 0), hs, w_out[0])

    hp = ffn(hp, g_ffn[0], w_ffn_gu[0].astype(BF16), w_ffn_down[0].astype(BF16))
    hs = ffn(hs, g_ffn[0], w_ffn_gu[0], w_ffn_down[0])

    w = w_in_gla[0]
    lr_off = GLA_R_OFF + GLA_DV
    w_in = jnp.concatenate([w[:, :lr_off], w[:, lr_off + GLA_GATE_RANK:], w[:, lr_off:lr_off + GLA_GATE_RANK]],
                           axis=1)
    mk1, mv1, mkt, mvh = mem_kv(1)
    zp = norm_matmul(hp, g_mix[1], w_in.astype(BF16))
    tok_p, st_p = gla_prompt(zp.reshape(B, T, -1), w_gate_up[0], b_gate[0], g_gla_onorm[0])
    xa_p = cross_prompt(zp[:, GLA_XQ_OFF2:GLA_XQ_OFF2 + X_W], mkt, mvh, B)
    hp = outproj(tok_p.reshape(B * T, GLA_DV), xa_p, hp, w_out[1].astype(BF16))

    zs = norm_matmul(hs, g_mix[1], w_in)
    tok_s, st_s = gla_sample(zs, w_gate_up[0], b_gate[0], g_gla_onorm[0],
                             state_gla[0].reshape(Bs, GLA_DK, GLA_HDV))
    hs = outproj(tok_s, cross_s(zs[:, GLA_XQ_OFF2:GLA_XQ_OFF2 + X_W], 1), hs, w_out[1])

    w_gu = w_exp_gu[0].astype(BF16)
    w_dn = w_exp_down[0].astype(BF16)
    y_p = moe_final(hp, g_ffn[1], w_router[0], w_gu, w_dn, g_final, tm=512)
    y_s = moe_final(hs, g_ffn[1], w_router[0], w_gu, w_dn, g_final, tm=128)

    st_p = jnp.stack([jnp.stack([st_p[b, h * GLA_HDV:(h + 1) * GLA_HDV, h * GLA_HDK:(h + 1) * GLA_HDK].T
                                 for h in range(GLA_HEADS)]) for b in range(B)])
    return (y_p.reshape(B, T, D), y_s.reshape(Bs, 1, D),
            k_p.reshape(1, B, T, SB_HEADS, HEAD_DIM), v_p.reshape(1, B, T, SB_HEADS, HEAD_DIM),
            k_s.reshape(1, Bs, 1, SB_HEADS, HEAD_DIM), v_s.reshape(1, Bs, 1, SB_HEADS, HEAD_DIM),
            st_p[None], st_s.reshape(1, Bs, GLA_HEADS, GLA_HDK, GLA_HDV),
            jnp.stack([mk0, mk1]), jnp.stack([mv0, mv1]))
```

```python
import functools

import jax
import jax.numpy as jnp
from jax import lax
from jax.experimental import pallas as pl
from jax.experimental.pallas import tpu as pltpu

F32 = jnp.float32
BF16 = jnp.bfloat16

D_MODEL = 1024
HEAD_DIM = 64
X_HEADS = 4
X_W = X_HEADS * HEAD_DIM
TOK_W = D_MODEL - X_W
SB_HEADS = TOK_W // HEAD_DIM
GLA_HEADS = 4
GLA_DV = TOK_W
GLA_DK = GLA_DV // 2
GLA_HDK = GLA_DK // GLA_HEADS
GLA_HDV = GLA_DV // GLA_HEADS
GLA_GATE_RANK = 16
GLA_TAU = 16.0
GLA_CHUNK = 64
GLA_SUB = 16
N_EXPERTS = 8
RMS_EPS = 1e-6
PAGE_SIZE = 128

LANES = 128
SB_KB = 256
VMEM_LIMIT = 56 * 1024 * 1024


def _cparams(sem):
    return pltpu.CompilerParams(dimension_semantics=sem, vmem_limit_bytes=VMEM_LIMIT)


def _rms(x, g):
    return x * lax.rsqrt(jnp.mean(x * x, axis=-1, keepdims=True) + RMS_EPS) * g


def _dot(a, b):
    return jnp.dot(a, b, preferred_element_type=F32)


def _split_bf16(x):
    hi = x.astype(BF16)
    lo = (x - hi.astype(F32)).astype(BF16)
    return hi, lo


def _dot_split(a, b):
    ah, al = _split_bf16(a)
    bh, bl = _split_bf16(b)
    return _dot(ah, bh) + (_dot(al, bh) + _dot(ah, bl))


def _matmul(a, w):
    if w.dtype == BF16:
        return _dot(a.astype(BF16), w)
    return _dot_split(a, w)


def _dot_nt(a, b):
    return lax.dot_general(a, b, (((1,), (1,)), ((), ())), preferred_element_type=F32)


def _dot_tn(a, b):
    return lax.dot_general(a, b, (((0,), (0,)), ((), ())), preferred_element_type=F32)


def _softplus(z):
    return jnp.maximum(z, 0.0) + jnp.log(1.0 + jnp.exp(-jnp.abs(z)))


def _tile(n, pref):
    t = min(n, pref)
    assert n % t == 0, (n, t)
    return t


def _norm_matmul_kernel(x_ref, g_ref, w_ref, o_ref):
    o_ref[...] = _matmul(_rms(x_ref[...], g_ref[...]), w_ref[...])


def norm_matmul(x, g, w_bf16, tm=512):
    n, d = x.shape
    m = w_bf16.shape[1]
    tm = _tile(n, tm)
    return pl.pallas_call(
        _norm_matmul_kernel,
        grid=(n // tm,),
        in_specs=[pl.BlockSpec((tm, d), lambda i: (i, 0)),
                  pl.BlockSpec((1, d), lambda i: (0, 0)),
                  pl.BlockSpec((d, m), lambda i: (0, 0))],
        out_specs=pl.BlockSpec((tm, m), lambda i: (i, 0)),
        out_shape=jax.ShapeDtypeStruct((n, m), F32),
        compiler_params=_cparams(("parallel",)),
        name="norm_matmul",
    )(x, g.reshape(1, d), w_bf16)


def _sb_inproj_kernel(x_ref, g_ref, w_ref, qt_ref, kh_ref, vt_ref, k_ref, v_ref, xq_ref):
    tm = x_ref.shape[0]
    xn = _rms(x_ref[...], g_ref[...]).astype(BF16)
    z = _dot(xn, w_ref[...])
    k_ref[...] = z[:, TOK_W:2 * TOK_W]
    v_ref[...] = z[:, 2 * TOK_W:3 * TOK_W]
    xq_ref[...] = z[:, 3 * TOK_W:]
    for h in range(SB_HEADS):
        lo = TOK_W + h * HEAD_DIM
        kh_ref[h] = z[:, lo:lo + HEAD_DIM].astype(BF16)
    for p in range(SB_HEADS // 2):
        qt2 = (z[:, p * LANES:(p + 1) * LANES] * (HEAD_DIM ** -0.5)).T
        vt2 = z[:, 2 * TOK_W + p * LANES:2 * TOK_W + (p + 1) * LANES].T
        for hh in range(2):
            rows = slice(hh * HEAD_DIM, (hh + 1) * HEAD_DIM)
            for c in range(tm // SB_KB):
                cols = slice(c * SB_KB, (c + 1) * SB_KB)
                qt_ref[2 * p + hh, c] = qt2[rows, cols].astype(BF16)
                vt_ref[2 * p + hh, c] = vt2[rows, cols].astype(BF16)


def sb_inproj(x, g, w_bf16, tm=512):
    n, d = x.shape
    m = w_bf16.shape[1]
    tm = _tile(n, tm)
    nkb = tm // SB_KB
    t_spec = pl.BlockSpec((SB_HEADS, nkb, HEAD_DIM, SB_KB), lambda i: (0, i, 0, 0))
    t_shape = jax.ShapeDtypeStruct((SB_HEADS, n // SB_KB, HEAD_DIM, SB_KB), BF16)
    return pl.pallas_call(
        _sb_inproj_kernel,
        grid=(n // tm,),
        in_specs=[pl.BlockSpec((tm, d), lambda i: (i, 0)),
                  pl.BlockSpec((1, d), lambda i: (0, 0)),
                  pl.BlockSpec((d, m), lambda i: (0, 0))],
        out_specs=[t_spec,
                   pl.BlockSpec((SB_HEADS, tm, HEAD_DIM), lambda i: (0, i, 0)),
                   t_spec,
                   pl.BlockSpec((tm, TOK_W), lambda i: (i, 0)),
                   pl.BlockSpec((tm, TOK_W), lambda i: (i, 0)),
                   pl.BlockSpec((tm, X_W), lambda i: (i, 0))],
        out_shape=[t_shape,
                   jax.ShapeDtypeStruct((SB_HEADS, n, HEAD_DIM), BF16),
                   t_shape,
                   jax.ShapeDtypeStruct((n, TOK_W), F32),
                   jax.ShapeDtypeStruct((n, TOK_W), F32),
                   jax.ShapeDtypeStruct((n, X_W), F32)],
        compiler_params=_cparams(("parallel",)),
        name="sb_inproj",
    )(x, g.reshape(1, d), w_bf16)


LOG2E = 1.4426950408889634


def _sb_prompt_kernel(bias_ref, qt_ref, k_ref, vt_ref, o_ref, z_sc, t_sc, sp_sc, a_sc):
    hp = pl.program_id(1)
    i = pl.program_id(2)
    tq = qt_ref.shape[-1]
    row = lax.broadcasted_iota(jnp.int32, (SB_KB, SB_KB), 0)
    col = lax.broadcasted_iota(jnp.int32, (SB_KB, SB_KB), 1)
    tri = (col > row).astype(BF16)
    causal = row < col
    heads = range(2)

    def scores(j, hh):
        k = k_ref[hh, pl.ds(pl.multiple_of(j * SB_KB, SB_KB), SB_KB), :]
        return _dot(k, qt_ref[hh, 0]) + bias_ref[2 * hp + hh]

    def stage1(z, hh, slot, mask):
        e = jnp.exp2(jnp.abs(z) * (-LOG2E))
        sp = jnp.maximum(z, 0.0) + jnp.log(1.0 + e)
        t = z - sp
        if mask is not None:
            sp = jnp.where(mask, sp, 0.0)
            t = jnp.where(mask, t, -jnp.inf)
        t_sc[slot, hh] = t
        sp_sc[hh] = sp.astype(BF16)
        return sp[0:8, :].astype(BF16).astype(F32)

    def tick(j1, slot, state, with_stage1):
        heads8, ws, carries, accs = state
        j3 = jnp.minimum(j1 + 2, i)
        pvs = [_dot(vt_ref[hh, j3], a_sc[hh]) for hh in heads]
        locs = [_dot(tri, sp_sc[hh]) for hh in heads]
        heads8_next = heads8
        if with_stage1:
            zs_next = [scores(jnp.maximum(j1 - 1, 0), hh) for hh in heads]
            heads8_next = [stage1(z_sc[slot, hh], hh, 1 - slot, None) for hh in heads]
            for hh in heads:
                z_sc[1 - slot, hh] = zs_next[hh]
        ws_next, new_c, new_a = [], [], []
        for hh in heads:
            a_sc[hh] = jnp.exp(t_sc[slot, hh] - locs[hh]).astype(BF16)
            ws_next.append(jnp.exp(-carries[hh]))
            total = locs[hh][0:1, :] + heads8[hh][0:1, :]
            new_c.append(carries[hh] + jnp.broadcast_to(total, carries[hh].shape))
            new_a.append(accs[hh] + ws[hh][0:1, :] * pvs[hh])
        return heads8_next, ws_next, new_c, new_a

    heads8 = [stage1(scores(i, hh), hh, 0, causal) for hh in heads]
    for hh in heads:
        z_sc[0, hh] = scores(jnp.maximum(i - 1, 0), hh)
        a_sc[hh] = jnp.zeros((SB_KB, tq), BF16)
    zrow = [jnp.minimum(h8, 0.0) for h8 in heads8]
    state = (heads8, zrow, zrow, [jnp.zeros((HEAD_DIM, tq), F32) for _ in heads])
    state = lax.fori_loop(0, i, lambda jj, st: tick(i - 1 - jj, jj & 1, st, True), state)
    _, ws, _, accs = tick(-1, i & 1, state, False)
    accs = [accs[hh] + ws[hh][0:1, :] * _dot(vt_ref[hh, 0], a_sc[hh]) for hh in heads]
    o_ref[...] = jnp.concatenate(accs, axis=0).T


def sb_prompt_attention(bias, qt, kh, vt, batch):
    n = kh.shape[1]
    t = n // batch
    tq = SB_KB
    nq = t // tq
    return pl.pallas_call(
        _sb_prompt_kernel,
        grid=(batch, SB_HEADS // 2, nq),
        in_specs=[pl.BlockSpec(memory_space=pltpu.SMEM),
                  pl.BlockSpec((2, 1, HEAD_DIM, tq), lambda b, h, i: (h, b * nq + i, 0, 0)),
                  pl.BlockSpec((2, t, HEAD_DIM), lambda b, h, i: (h, b, 0)),
                  pl.BlockSpec((2, nq, HEAD_DIM, SB_KB), lambda b, h, i: (h, b, 0, 0))],
        out_specs=pl.BlockSpec((tq, 2 * HEAD_DIM), lambda b, h, i: (b * nq + i, h)),
        out_shape=jax.ShapeDtypeStruct((n, TOK_W), F32),
        scratch_shapes=[pltpu.VMEM((2, 2, SB_KB, tq), F32), pltpu.VMEM((2, 2, SB_KB, tq), F32),
                        pltpu.VMEM((2, SB_KB, tq), BF16), pltpu.VMEM((2, SB_KB, tq), BF16)],
        compiler_params=_cparams(("parallel", "parallel", "arbitrary")),
        name="sb_prompt_attention",
    )(bias, qt, kh, vt)


SB_PAGES_PER_STEP = 16


def _split3_bf16(x):
    hi = x.astype(BF16)
    r = x - hi.astype(F32)
    mid = r.astype(BF16)
    lo = (r - mid.astype(F32)).astype(BF16)
    return hi, mid, lo


def _sb_sample_kernel(pt_ref, q_ref, bias_ref, *refs):
    del pt_ref
    npg = (len(refs) - 3) // 2
    k_refs, v_refs = refs[:npg], refs[npg:2 * npg]
    o_ref, carry_sc, acc_sc = refs[2 * npg:]
    g = pl.program_id(1)
    hpad = carry_sc.shape[0]

    @pl.when(g == 0)
    def _():
        carry_sc[...] = jnp.zeros_like(carry_sc)
        acc_sc[...] = jnp.zeros_like(acc_sc)

    r = lax.broadcasted_iota(jnp.int32, (PAGE_SIZE, PAGE_SIZE), 0)
    c = lax.broadcasted_iota(jnp.int32, (PAGE_SIZE, PAGE_SIZE), 1)
    tri = (r > c).astype(BF16)
    qs = [_row_to_col(q_ref[0, h:h + 1, :]) * (HEAD_DIM ** -0.5) for h in range(SB_HEADS)]
    pad = jnp.zeros((hpad - SB_HEADS, PAGE_SIZE), F32)
    carry = carry_sc[...]
    for p in range(npg):
        rows = [jnp.sum(k_refs[p][0, h] * qs[h], axis=0, keepdims=True) for h in range(SB_HEADS)]
        z = jnp.concatenate(rows + [pad], axis=0) + bias_ref[...]
        sp = _softplus(z)
        hi, mid, lo = _split3_bf16(sp)
        loc = _dot(hi, tri) + (_dot(mid, tri) + _dot(lo, tri))
        a = jnp.exp(z - sp - loc - carry)
        for h in range(SB_HEADS):
            acc_sc[h] += a[h:h + 1, :] * v_refs[p][0, h]
        carry = carry + jnp.broadcast_to(loc[:, 0:1] + sp[:, 0:1], carry.shape)
    carry_sc[...] = carry

    @pl.when(g == pl.num_programs(1) - 1)
    def _():
        o_ref[0] = jnp.concatenate(
            [_col_to_row(jnp.sum(acc_sc[h], axis=-1, keepdims=True)) for h in range(SB_HEADS)], axis=0)


def sb_sample_attention(q, bias, cache_k, cache_v, page_table):
    bs = q.shape[0]
    n_pages = page_table.shape[1]
    npg = _tile(n_pages, SB_PAGES_PER_STEP)
    hpad = 16
    bias_b = jnp.zeros((hpad, PAGE_SIZE), F32).at[:SB_HEADS].set(
        jnp.broadcast_to(bias[:, None], (SB_HEADS, PAGE_SIZE)))
    pt = page_table.reshape(-1)

    def page_spec(p):
        def index_map(b, g, pt_ref):
            return (pt_ref[b * n_pages + (n_pages - 1 - g * npg - p)], 0, 0, 0)
        return pl.BlockSpec((1, SB_HEADS, HEAD_DIM, PAGE_SIZE), index_map)

    qo_spec = pl.BlockSpec((1, SB_HEADS, HEAD_DIM), lambda b, g, pt_ref: (b, 0, 0))
    grid_spec = pltpu.PrefetchScalarGridSpec(
        num_scalar_prefetch=1,
        grid=(bs, n_pages // npg),
        in_specs=[qo_spec, pl.BlockSpec((hpad, PAGE_SIZE), lambda b, g, pt_ref: (0, 0))]
        + [page_spec(p) for p in range(npg)] * 2,
        out_specs=qo_spec,
        scratch_shapes=[pltpu.VMEM((hpad, PAGE_SIZE), F32), pltpu.VMEM((SB_HEADS, HEAD_DIM, PAGE_SIZE), F32)],
    )
    out = pl.pallas_call(
        _sb_sample_kernel,
        grid_spec=grid_spec,
        out_shape=jax.ShapeDtypeStruct((bs, SB_HEADS, HEAD_DIM), F32),
        compiler_params=_cparams(("parallel", "arbitrary")),
        name="sb_sample_attention",
    )(pt, q.reshape(bs, SB_HEADS, HEAD_DIM), bias_b, *([cache_k] * npg), *([cache_v] * npg))
    return out.reshape(bs, TOK_W)


def _softmax_rows(s):
    m = jnp.max(s, axis=-1, keepdims=True)
    e = jnp.exp(s - m)
    return e / jnp.sum(e, axis=-1, keepdims=True)


def _cross_prompt_kernel(q_ref, mkt_ref, mv_ref, o_ref):
    q = q_ref[...]
    for h in range(X_HEADS):
        lo = h * HEAD_DIM
        qh = q[:, lo:lo + HEAD_DIM].astype(BF16)
        s = _dot(qh, mkt_ref[0, h]) * (HEAD_DIM ** -0.5)
        p = _softmax_rows(s).astype(BF16)
        o_ref[:, lo:lo + HEAD_DIM] = _dot(p, mv_ref[0, h])


def cross_prompt(xq, mkt, mvh, batch, tq=512):
    n = xq.shape[0]
    t = n // batch
    tq = _tile(t, tq)
    nq = t // tq
    nm = mkt.shape[-1]
    return pl.pallas_call(
        _cross_prompt_kernel,
        grid=(batch, nq),
        in_specs=[pl.BlockSpec((tq, X_W), lambda b, i: (b * nq + i, 0)),
                  pl.BlockSpec((1, X_HEADS, HEAD_DIM, nm), lambda b, i: (b, 0, 0, 0)),
                  pl.BlockSpec((1, X_HEADS, nm, HEAD_DIM), lambda b, i: (b, 0, 0, 0))],
        out_specs=pl.BlockSpec((tq, X_W), lambda b, i: (b * nq + i, 0)),
        out_shape=jax.ShapeDtypeStruct((n, X_W), F32),
        compiler_params=_cparams(("parallel", "parallel")),
        name="cross_prompt",
    )(xq, mkt, mvh)


def _eye(n):
    return lax.broadcasted_iota(jnp.int32, (n, n), 0) == lax.broadcasted_iota(jnp.int32, (n, n), 1)


def _row_to_col(row):
    n = row.shape[1]
    return jnp.sum(jnp.where(_eye(n), jnp.broadcast_to(row, (n, n)), 0.0), axis=1, keepdims=True)


def _col_to_row(col):
    n = col.shape[0]
    return jnp.sum(jnp.where(_eye(n), jnp.broadcast_to(col, (n, n)), 0.0), axis=0, keepdims=True)


def _cross_sample_kernel(q_ref, mk_ref, mv_ref, o_ref):
    for j in range(q_ref.shape[0]):
        rows = []
        for h in range(X_HEADS):
            q = _row_to_col(q_ref[j, h:h + 1, :])
            s = jnp.sum(mk_ref[0, j, h] * q, axis=0, keepdims=True) * (HEAD_DIM ** -0.5)
            p = _softmax_rows(s)
            rows.append(_col_to_row(jnp.sum(mv_ref[0, j, h] * p, axis=-1, keepdims=True)))
        o_ref[j] = jnp.concatenate(rows, axis=0)


def cross_sample(xq, mk, mv, layer, g=8):
    bs = xq.shape[0]
    g = _tile(bs, g)
    nm = mk.shape[-1]
    qo_spec = pl.BlockSpec((g, X_HEADS, HEAD_DIM), lambda i: (i, 0, 0))
    m_spec = pl.BlockSpec((1, g, X_HEADS, HEAD_DIM, nm), lambda i: (layer, i, 0, 0, 0))
    out = pl.pallas_call(
        _cross_sample_kernel,
        grid=(bs // g,),
        in_specs=[qo_spec, m_spec, m_spec],
        out_specs=qo_spec,
        out_shape=jax.ShapeDtypeStruct((bs, X_HEADS, HEAD_DIM), F32),
        compiler_params=_cparams(("parallel",)),
        name="cross_sample",
    )(xq.reshape(bs, X_HEADS, HEAD_DIM), mk, mv)
    return out.reshape(bs, X_W)


def _outproj_kernel(tok_ref, xa_ref, h_ref, wt_ref, wb_ref, o_ref):
    o_ref[...] = h_ref[...] + _matmul(tok_ref[...], wt_ref[...]) + _matmul(xa_ref[...], wb_ref[...])


def outproj(tok, xa, h, w_bf16, tm=512):
    n, d = h.shape
    tm = _tile(n, tm)
    wt, wb = w_bf16[:TOK_W], w_bf16[TOK_W:]
    return pl.pallas_call(
        _outproj_kernel,
        grid=(n // tm,),
        in_specs=[pl.BlockSpec((tm, TOK_W), lambda i: (i, 0)),
                  pl.BlockSpec((tm, X_W), lambda i: (i, 0)),
                  pl.BlockSpec((tm, d), lambda i: (i, 0)),
                  pl.BlockSpec((TOK_W, d), lambda i: (0, 0)),
                  pl.BlockSpec((X_W, d), lambda i: (0, 0))],
        out_specs=pl.BlockSpec((tm, d), lambda i: (i, 0)),
        out_shape=jax.ShapeDtypeStruct((n, d), F32),
        compiler_params=_cparams(("parallel",)),
        name="outproj",
    )(tok, xa, h, wt, wb)


def _ffn_kernel(x_ref, g_ref, wg_ref, wu_ref, wd_ref, o_ref, xn_sc, acc_sc):
    f = pl.program_id(1)

    @pl.when(f == 0)
    def _():
        xn_sc[...] = _rms(x_ref[...], g_ref[...]).astype(xn_sc.dtype)
        acc_sc[...] = jnp.zeros_like(acc_sc)

    xn = xn_sc[...]
    gate = _matmul(xn, wg_ref[...])
    up = _matmul(xn, wu_ref[...])
    acc_sc[...] += _matmul(jax.nn.silu(gate) * up, wd_ref[...])

    @pl.when(f == pl.num_programs(1) - 1)
    def _():
        o_ref[...] = x_ref[...] + acc_sc[...]


def ffn(x, g, w_gu, w_down, tm=1024, tf=256):
    n, d = x.shape
    dff = w_down.shape[0]
    tm = _tile(n, tm)
    nf = dff // tf
    return pl.pallas_call(
        _ffn_kernel,
        grid=(n // tm, nf),
        in_specs=[pl.BlockSpec((tm, d), lambda i, f: (i, 0)),
                  pl.BlockSpec((1, d), lambda i, f: (0, 0)),
                  pl.BlockSpec((d, tf), lambda i, f: (0, f)),
                  pl.BlockSpec((d, tf), lambda i, f: (0, f + nf)),
                  pl.BlockSpec((tf, d), lambda i, f: (f, 0))],
        out_specs=pl.BlockSpec((tm, d), lambda i, f: (i, 0)),
        out_shape=jax.ShapeDtypeStruct((n, d), F32),
        scratch_shapes=[pltpu.VMEM((tm, d), w_gu.dtype), pltpu.VMEM((tm, d), F32)],
        compiler_params=_cparams(("parallel", "arbitrary")),
        name="ffn",
    )(x, g.reshape(1, d), w_gu, w_gu, w_down)


def _router_kernel(x_ref, g_ref, w_ref, o_ref, cnt_ref, cnt_sc):
    @pl.when(pl.program_id(0) == 0)
    def _():
        cnt_sc[...] = jnp.zeros_like(cnt_sc)

    xn = _rms(x_ref[...], g_ref[...])
    logits = _dot_split(xn, w_ref[...])
    lane = lax.broadcasted_iota(jnp.int32, logits.shape, 1)
    neg = jnp.float32(-jnp.inf)
    logits = jnp.where(lane < N_EXPERTS, logits, neg)
    m1 = jnp.max(logits, axis=-1, keepdims=True)
    i1 = jnp.min(jnp.where(logits == m1, lane, LANES), axis=-1, keepdims=True)
    rest = jnp.where(lane == i1, neg, logits)
    m2 = jnp.max(rest, axis=-1, keepdims=True)
    i2 = jnp.min(jnp.where(rest == m2, lane, LANES), axis=-1, keepdims=True)
    e2 = jnp.exp(m2 - m1)
    den = 1.0 + e2
    oh1 = lane == i1
    oh2 = lane == i2
    cnt = (oh1 | oh2).astype(BF16)
    tm = cnt.shape[0]
    r = lax.broadcasted_iota(jnp.int32, (tm, tm), 0)
    c = lax.broadcasted_iota(jnp.int32, (tm, tm), 1)
    before = _dot((c < r).astype(BF16), cnt) + cnt_sc[...]
    r1 = jnp.sum(jnp.where(oh1, before, 0.0), axis=-1, keepdims=True)
    r2 = jnp.sum(jnp.where(oh2, before, 0.0), axis=-1, keepdims=True)
    cnt_sc[...] = before[tm - 1:tm, :] + cnt[tm - 1:tm, :].astype(F32)
    cnt_ref[...] = cnt_sc[...]
    info = jnp.zeros(logits.shape, F32)
    for k, val in enumerate((i1.astype(F32), i2.astype(F32), 1.0 / den, e2 / den, r1, r2)):
        info = jnp.where(lane == k, val, info)
    o_ref[...] = info


R_E1, R_E2, R_G1, R_G2, R_R1, R_R2 = range(6)


def router(x, g, w_router, tm=512):
    n, d = x.shape
    tm = _tile(n, tm)
    w_pad = jnp.zeros((d, LANES), F32).at[:, :N_EXPERTS].set(w_router)
    return pl.pallas_call(
        _router_kernel,
        grid=(n // tm,),
        in_specs=[pl.BlockSpec((tm, d), lambda i: (i, 0)),
                  pl.BlockSpec((1, d), lambda i: (0, 0)),
                  pl.BlockSpec((d, LANES), lambda i: (0, 0))],
        out_specs=[pl.BlockSpec((tm, LANES), lambda i: (i, 0)),
                   pl.BlockSpec((1, LANES), lambda i: (0, 0))],
        out_shape=[jax.ShapeDtypeStruct((n, LANES), F32), jax.ShapeDtypeStruct((1, LANES), F32)],
        scratch_shapes=[pltpu.VMEM((1, LANES), F32)],
        compiler_params=_cparams(("arbitrary",)),
        name="router",
    )(x, g.reshape(1, d), w_pad)


def _slot_sources_kernel(s1_ref, s2_ref, src_ref):
    def clear(s, carry):
        src_ref[s] = 0
        return carry

    def put(t, carry):
        src_ref[s1_ref[t]] = t
        src_ref[s2_ref[t]] = t
        return carry

    lax.fori_loop(0, src_ref.shape[0], clear, 0, unroll=8)
    lax.fori_loop(0, s1_ref.shape[0], put, 0, unroll=8)


def slot_sources(slot1, slot2, n_slots):
    return pl.pallas_call(
        _slot_sources_kernel,
        in_specs=[pl.BlockSpec(memory_space=pltpu.SMEM), pl.BlockSpec(memory_space=pltpu.SMEM)],
        out_specs=pl.BlockSpec(memory_space=pltpu.SMEM),
        out_shape=jax.ShapeDtypeStruct((n_slots,), jnp.int32),
        name="slot_sources",
    )(slot1, slot2)


def _row_copy(src_hbm, src_row, dst_ref, dst_row, sem):
    return pltpu.make_async_copy(src_hbm.at[pl.ds(src_row, 1)], dst_ref.at[pl.ds(dst_row, 1)], sem)


def _expert_kernel(te_ref, nv_ref, src_ref, x_hbm, g_ref, wg_ref, wu_ref, wd_ref, o_ref, xbuf, xn_sc, acc_sc, sem):
    del te_ref
    t = pl.program_id(0)
    f = pl.program_id(1)
    nf = pl.num_programs(1)
    tm = xbuf.shape[1]
    n_valid = nv_ref[0]
    half = t % 2

    def fetch(tile, first_row, n_rows, dst_half):
        for j in range(n_rows):
            r = first_row + j
            _row_copy(x_hbm, src_ref[tile * tm + r], xbuf.at[dst_half], r, sem.at[dst_half]).start(priority=j % 2)

    @pl.when((t == 0) & (f == 0))
    def _():
        fetch(0, 0, tm, 0)

    @pl.when((f == 0) & (t <= n_valid))
    def _():
        def wait(r, carry):
            _row_copy(x_hbm, 0, xbuf.at[half], r, sem.at[half]).wait()
            return carry
        lax.fori_loop(0, tm, wait, 0, unroll=8)

    @pl.when(t < n_valid)
    def _():
        @pl.when(f == 0)
        def _():
            xn_sc[...] = _rms(xbuf[half], g_ref[...]).astype(BF16)
            acc_sc[...] = jnp.zeros_like(acc_sc)

        xn = xn_sc[...]
        gate = _dot(xn, wg_ref[0])
        up = _dot(xn, wu_ref[0])
        act = (jax.nn.silu(gate) * up).astype(BF16)
        part = tm // nf
        fetch(t + 1, f * part, part, 1 - half)
        acc_sc[...] += _dot(act, wd_ref[0])

        @pl.when(f == nf - 1)
        def _():
            o_ref[...] = acc_sc[...]

    @pl.when((t >= n_valid) & (f == nf - 1))
    def _():
        o_ref[...] = jnp.zeros_like(o_ref)


def expert_swiglu(x, g, src, tile_expert, n_valid, w_gu_bf16, w_down_bf16, tm, tf=896):
    n_slots = src.shape[0]
    d = x.shape[1]
    dff = w_down_bf16.shape[1]
    nf = dff // tf
    assert tm % nf == 0

    def wmap(off):
        def index_map(t, f, te_ref, nv_ref, src_ref):
            return (te_ref[t], 0, jnp.where(t < nv_ref[0], f, nf - 1) + off)
        return index_map

    def dmap(t, f, te_ref, nv_ref, src_ref):
        return (te_ref[t], jnp.where(t < nv_ref[0], f, nf - 1), 0)

    grid_spec = pltpu.PrefetchScalarGridSpec(
        num_scalar_prefetch=3,
        grid=(n_slots // tm, nf),
        in_specs=[pl.BlockSpec(memory_space=pl.ANY),
                  pl.BlockSpec((1, d), lambda t, f, te_ref, nv_ref, src_ref: (0, 0)),
                  pl.BlockSpec((1, d, tf), wmap(0)),
                  pl.BlockSpec((1, d, tf), wmap(nf)),
                  pl.BlockSpec((1, tf, d), dmap)],
        out_specs=pl.BlockSpec((tm, d), lambda t, f, te_ref, nv_ref, src_ref: (t, 0)),
        scratch_shapes=[pltpu.VMEM((2, tm, d), F32), pltpu.VMEM((tm, d), BF16), pltpu.VMEM((tm, d), F32),
                        pltpu.SemaphoreType.DMA((2,))],
    )
    return pl.pallas_call(
        _expert_kernel,
        grid_spec=grid_spec,
        out_shape=jax.ShapeDtypeStruct((n_slots, d), F32),
        compiler_params=_cparams(("arbitrary", "arbitrary")),
        name="expert_swiglu",
    )(tile_expert, n_valid, src, x, g.reshape(1, d), w_gu_bf16, w_gu_bf16, w_down_bf16)


def _combine_kernel(s1_ref, s2_ref, x_ref, info_ref, gf_ref, ys_hbm, o_ref, buf, sem):
    tc = x_ref.shape[0]
    base = pl.program_id(0) * tc

    def start(r, carry):
        _row_copy(ys_hbm, s1_ref[base + r], buf.at[0], r, sem).start(priority=0)
        _row_copy(ys_hbm, s2_ref[base + r], buf.at[1], r, sem).start(priority=1)
        return carry

    def wait(r, carry):
        _row_copy(ys_hbm, 0, buf.at[0], r, sem).wait()
        _row_copy(ys_hbm, 0, buf.at[1], r, sem).wait()
        return carry

    lax.fori_loop(0, tc, start, 0, unroll=8)
    lax.fori_loop(0, tc, wait, 0, unroll=8)
    info = info_ref[...]
    g1 = info[:, R_G1:R_G1 + 1]
    g2 = info[:, R_G2:R_G2 + 1]
    o_ref[...] = _rms(x_ref[...] + (g1 * buf[0] + g2 * buf[1]), gf_ref[...])


def combine_final(x, info, slot1, slot2, ys, g_final, tc=256):
    n, d = x.shape
    tc = _tile(n, tc)
    grid_spec = pltpu.PrefetchScalarGridSpec(
        num_scalar_prefetch=2,
        grid=(n // tc,),
        in_specs=[pl.BlockSpec((tc, d), lambda i, s1, s2: (i, 0)),
                  pl.BlockSpec((tc, LANES), lambda i, s1, s2: (i, 0)),
                  pl.BlockSpec((1, d), lambda i, s1, s2: (0, 0)),
                  pl.BlockSpec(memory_space=pl.ANY)],
        out_specs=pl.BlockSpec((tc, d), lambda i, s1, s2: (i, 0)),
        scratch_shapes=[pltpu.VMEM((2, tc, d), F32), pltpu.SemaphoreType.DMA(())],
    )
    return pl.pallas_call(
        _combine_kernel,
        grid_spec=grid_spec,
        out_shape=jax.ShapeDtypeStruct((n, d), F32),
        compiler_params=_cparams(("arbitrary",)),
        name="combine_final",
    )(slot1, slot2, x, info, g_final.reshape(1, d), ys)


def moe_final(x, g, w_router, w_gu_bf16, w_down_bf16, g_final, tm):
    n, d = x.shape
    info, counts = router(x, g, w_router)
    counts = counts[0, :N_EXPERTS].astype(jnp.int32)
    padded = ((counts + tm - 1) // tm) * tm
    ends = jnp.cumsum(padded)
    offs = ends - padded
    n_slots = (-(-2 * n // tm) + N_EXPERTS) * tm
    e1 = info[:, R_E1].astype(jnp.int32)
    e2 = info[:, R_E2].astype(jnp.int32)
    slot1 = offs[e1] + info[:, R_R1].astype(jnp.int32)
    slot2 = offs[e2] + info[:, R_R2].astype(jnp.int32)
    src = slot_sources(slot1, slot2, n_slots)
    tile_start = jnp.arange(n_slots // tm, dtype=jnp.int32) * tm
    n_valid = (ends[-1] // tm).reshape(1)
    tile_expert = jnp.minimum(jnp.sum(tile_start[:, None] >= ends[None, :], axis=1), N_EXPERTS - 1).astype(jnp.int32)
    tile_expert = jnp.where(tile_start < ends[-1], tile_expert, tile_expert[jnp.maximum(n_valid[0] - 1, 0)])
    ys = expert_swiglu(x, g, src, tile_expert, n_valid, w_gu_bf16, w_down_bf16, tm)
    return combine_final(x, info, slot1, slot2, ys, g_final)


GLA_R_OFF = 2 * GLA_DK + GLA_DV
GLA_XQ_OFF2 = GLA_R_OFF + GLA_DV
GLA_LR_OFF2 = GLA_XQ_OFF2 + X_W


def _head_of(col, width):
    return ((col >= width).astype(jnp.int32) + (col >= 2 * width).astype(jnp.int32)
            + (col >= 3 * width).astype(jnp.int32))


def _gla_log_decay(lr, wgu, bg):
    gl = _matmul(lr, wgu) + bg
    return -_softplus(-gl) / GLA_TAU


def _gla_out(o, r, gon):
    col = lax.broadcasted_iota(jnp.int32, (1, GLA_DV), 1)
    hv = _head_of(col, GLA_HDV)
    o2 = o * o
    inv = jnp.zeros_like(o)
    for h in range(GLA_HEADS):
        m = hv == h
        ms = jnp.sum(jnp.where(m, o2, 0.0), axis=-1, keepdims=True) * (1.0 / GLA_HDV)
        inv = jnp.where(m, lax.rsqrt(ms + RMS_EPS), inv)
    return (o * inv * gon) * jax.nn.silu(r)


def _gla_prompt_kernel(z_ref, wgu_ref, bg_ref, gon_ref, bd_ref, tok_ref, st_ref, st_sc):
    nb = z_ref.shape[0]
    gt = z_ref.shape[1]
    c = GLA_CHUNK

    @pl.when(pl.program_id(0) == 0)
    def _():
        st_sc[...] = jnp.zeros_like(st_sc)

    rr = lax.broadcasted_iota(jnp.int32, (c, c), 0)
    cc = lax.broadcasted_iota(jnp.int32, (c, c), 1)
    ltri = (cc <= rr).astype(BF16)
    r4 = lax.broadcasted_iota(jnp.int32, (c, GLA_HEADS * c), 0)
    c4 = lax.broadcasted_iota(jnp.int32, (c, GLA_HEADS * c), 1)
    intra_mask = (c4 & (c - 1)) <= r4
    key_row = lax.broadcasted_iota(jnp.int32, (c, 1), 0)
    hk = _head_of(lax.broadcasted_iota(jnp.int32, (1, GLA_DK), 1), GLA_HDK)
    hv = _head_of(lax.broadcasted_iota(jnp.int32, (1, GLA_DV), 1), GLA_HDV)
    wgu = wgu_ref[...]
    bg = bg_ref[...]
    gon = gon_ref[...]
    bd = bd_ref[...]

    for ci in range(gt // c):
        for b in range(nb):
            zc = z_ref[b, ci * c:(ci + 1) * c, :]
            q = zc[:, 0:GLA_DK] * (GLA_HDK ** -0.5)
            k = zc[:, GLA_DK:2 * GLA_DK]
            v = zc[:, 2 * GLA_DK:GLA_R_OFF]
            r = zc[:, GLA_R_OFF:GLA_XQ_OFF2]
            lr = zc[:, GLA_LR_OFF2:GLA_LR_OFF2 + GLA_GATE_RANK]
            la = _gla_log_decay(lr, wgu, bg)
            hi = la.astype(BF16)
            r1 = la - hi.astype(F32)
            mid = r1.astype(BF16)
            low = (r1 - mid.astype(F32)).astype(BF16)
            bc = _dot(ltri, hi) + (_dot(ltri, mid) + _dot(ltri, low))
            qe = (q * jnp.exp(bc)).astype(BF16)
            vb = v.astype(BF16)
            v_stack = jnp.concatenate(
                [jnp.where(hv == h, vb, jnp.zeros_like(vb)) for h in range(GLA_HEADS)], axis=0)
            parts = []
            for g0 in range(0, c, GLA_SUB):
                ref = bc[g0 - 1:g0, :] if g0 else jnp.zeros((1, GLA_DK), F32)
                qg = (q[g0:g0 + GLA_SUB] * jnp.exp(bc[g0:g0 + GLA_SUB] - ref)).astype(BF16)
                kg = (k * jnp.exp(jnp.where(key_row < g0 + GLA_SUB, ref - bc, -jnp.inf))).astype(BF16)
                kg_stack = jnp.concatenate(
                    [jnp.where(hk == h, kg, jnp.zeros_like(kg)) for h in range(GLA_HEADS)], axis=0)
                parts.append(_dot_nt(qg, kg_stack))
            scores = jnp.where(intra_mask, jnp.concatenate(parts, axis=0), 0.0)
            o_intra = _dot(scores.astype(BF16), v_stack)
            st = st_sc[b]
            o_inter = _dot_nt(qe, st.astype(BF16))
            tok_ref[b, ci * c:(ci + 1) * c, :] = _gla_out(o_inter + o_intra, r, gon)
            b_last = bc[c - 1:c, :]
            kd = (k * jnp.exp(b_last - bc)).astype(BF16)
            st_sc[b] = st * jnp.exp(b_last) + _dot_tn(vb, kd) * bd

    @pl.when(pl.program_id(0) == pl.num_programs(0) - 1)
    def _():
        st_ref[...] = st_sc[...]


def gla_prompt(z, w_gate_up, b_gate, g_onorm, gt=256):
    nb, t, zw = z.shape
    gt = _tile(t, gt)
    rv = lax.broadcasted_iota(jnp.int32, (GLA_DV, GLA_DK), 0) // GLA_HDV
    ck = lax.broadcasted_iota(jnp.int32, (GLA_DV, GLA_DK), 1) // GLA_HDK
    bd = (rv == ck).astype(F32)
    return pl.pallas_call(
        _gla_prompt_kernel,
        grid=(t // gt,),
        in_specs=[pl.BlockSpec((nb, gt, zw), lambda i: (0, i, 0)),
                  pl.BlockSpec((GLA_GATE_RANK, GLA_DK), lambda i: (0, 0)),
                  pl.BlockSpec((1, GLA_DK), lambda i: (0, 0)),
                  pl.BlockSpec((1, GLA_DV), lambda i: (0, 0)),
                  pl.BlockSpec((GLA_DV, GLA_DK), lambda i: (0, 0))],
        out_specs=[pl.BlockSpec((nb, gt, GLA_DV), lambda i: (0, i, 0)),
                   pl.BlockSpec((nb, GLA_DV, GLA_DK), lambda i: (0, 0, 0))],
        out_shape=[jax.ShapeDtypeStruct((nb, t, GLA_DV), F32),
                   jax.ShapeDtypeStruct((nb, GLA_DV, GLA_DK), F32)],
        scratch_shapes=[pltpu.VMEM((nb, GLA_DV, GLA_DK), F32)],
        compiler_params=_cparams(("arbitrary",)),
        name="gla_prompt",
    )(z, w_gate_up.astype(BF16), b_gate.reshape(1, GLA_DK), jnp.tile(g_onorm, GLA_HEADS).reshape(1, GLA_DV), bd)


def _gla_gate_kernel(z_ref, wgu_ref, bg_ref, q_ref, ea_ref):
    z = z_ref[...]
    la = _gla_log_decay(z[:, GLA_LR_OFF2:GLA_LR_OFF2 + GLA_GATE_RANK], wgu_ref[...], bg_ref[...])
    ea_ref[...] = jnp.exp(la)
    q_ref[...] = z[:, 0:GLA_DK] * (GLA_HDK ** -0.5)


def _gla_sample_kernel(q_ref, k_ref, ea_ref, z_ref, gon_ref, s0_ref, tok_ref, s_ref):
    g = q_ref.shape[0]
    for j in range(g):
        zr = z_ref[j]
        v = zr[:, 2 * GLA_DK:GLA_R_OFF]
        r = zr[:, GLA_R_OFF:GLA_XQ_OFF2]
        v_rows = jnp.concatenate(
            [jnp.broadcast_to(v[:, h * GLA_HDV:(h + 1) * GLA_HDV], (GLA_HDK, GLA_HDV))
             for h in range(GLA_HEADS)], axis=0)
        s_new = ea_ref[j] * s0_ref[j] + k_ref[j] * v_rows
        s_ref[j] = s_new
        qs = q_ref[j] * s_new
        o = jnp.concatenate(
            [jnp.sum(qs[h * GLA_HDK:(h + 1) * GLA_HDK], axis=0, keepdims=True) for h in range(GLA_HEADS)],
            axis=1)
        tok_ref[j] = _gla_out(o, r, gon_ref[...])


def gla_sample(z, w_gate_up, b_gate, g_onorm, state0, g=8):
    bs, zw = z.shape
    g = _tile(bs, g)
    q, ea = pl.pallas_call(
        _gla_gate_kernel,
        out_shape=[jax.ShapeDtypeStruct((bs, GLA_DK), F32), jax.ShapeDtypeStruct((bs, GLA_DK), F32)],
        name="gla_gate",
    )(z, w_gate_up, b_gate.reshape(1, GLA_DK))
    k = z[:, GLA_DK:2 * GLA_DK]
    col = lambda a: a.reshape(bs, GLA_DK, 1)
    col_spec = pl.BlockSpec((g, GLA_DK, 1), lambda i: (i, 0, 0))
    tok, s_new = pl.pallas_call(
        _gla_sample_kernel,
        grid=(bs // g,),
        in_specs=[col_spec, col_spec, col_spec,
                  pl.BlockSpec((g, 1, zw), lambda i: (i, 0, 0)),
                  pl.BlockSpec((1, GLA_DV), lambda i: (0, 0)),
                  pl.BlockSpec((g, GLA_DK, GLA_HDV), lambda i: (i, 0, 0))],
        out_specs=[pl.BlockSpec((g, 1, GLA_DV), lambda i: (i, 0, 0)),
                   pl.BlockSpec((g, GLA_DK, GLA_HDV), lambda i: (i, 0, 0))],
        out_shape=[jax.ShapeDtypeStruct((bs, 1, GLA_DV), F32),
                   jax.ShapeDtypeStruct((bs, GLA_DK, GLA_HDV), F32)],
        compiler_params=_cparams(("parallel",)),
        name="gla_sample",
    )(col(q), col(k), col(ea), z.reshape(bs, 1, zw), jnp.tile(g_onorm, GLA_HEADS).reshape(1, GLA_DV), state0)
    return tok.reshape(bs, GLA_DV), s_new


def kernel(x_prompt, x_sample, cache_sb_k, cache_sb_v, state_gla, cache_mem_k, cache_mem_v, page_table, mem_prompt,
           g_mix, g_mem, w_mem_kv, w_in_sb, b_sb, w_in_gla, w_gate_up, b_gate, g_gla_onorm, w_out,
           g_ffn, w_ffn_gu, w_ffn_down, w_router, w_exp_gu, w_exp_down, g_final):
    B, T, D = x_prompt.shape
    Bs = x_sample.shape[0]
    n_mem = mem_prompt.shape[1]
    n_phys = cache_sb_k.shape[1]
    hp = x_prompt.reshape(B * T, D)
    hs = x_sample.reshape(Bs, D)
    mem = mem_prompt.reshape(B * n_mem, D)

    def mem_kv(i):
        mkv = norm_matmul(mem, g_mem[i], w_mem_kv[i].astype(BF16))
        mk = mkv[:, :X_W].reshape(B, n_mem, X_HEADS, HEAD_DIM)
        mv = mkv[:, X_W:].reshape(B, n_mem, X_HEADS, HEAD_DIM)
        mkt = jnp.transpose(mk, (0, 2, 3, 1)).astype(BF16)
        mvh = jnp.transpose(mv, (0, 2, 1, 3)).astype(BF16)
        return mk, mv, mkt, mvh

    def cross_s(xq, i):
        return cross_sample(xq, mem_k_t, mem_v_t, i)

    mem_k_t = jnp.transpose(cache_mem_k, (0, 1, 3, 4, 2))
    mem_v_t = jnp.transpose(cache_mem_v, (0, 1, 3, 4, 2))

    mk0, mv0, mkt, mvh = mem_kv(0)
    qt, kh, vt, k_p, v_p, xq_p = sb_inproj(hp, g_mix[0], w_in_sb[0].astype(BF16))
    tok_p = sb_prompt_attention(b_sb[0], qt, kh, vt, B)
    xa_p = cross_prompt(xq_p, mkt, mvh, B)
    hp = outproj(tok_p, xa_p, hp, w_out[0].astype(BF16))

    zs = norm_matmul(hs, g_mix[0], w_in_sb[0])
    k_s, v_s, xq_s = zs[:, TOK_W:2 * TOK_W], zs[:, 2 * TOK_W:3 * TOK_W], zs[:, 3 * TOK_W:]
    tok_s = sb_sample_attention(zs[:, :TOK_W], b_sb[0],
                                jnp.transpose(cache_sb_k[0], (0, 2, 3, 1)),
                                jnp.transpose(cache_sb_v[0], (0, 2, 3, 1)), page_table)
    hs = outproj(tok_s, cross_s(xq_s, 0), hs, w_out[0])

    hp = ffn(hp, g_ffn[0], w_ffn_gu[0].astype(BF16), w_ffn_down[0].astype(BF16))
    hs = ffn(hs, g_ffn[0], w_ffn_gu[0], w_ffn_down[0])

    w = w_in_gla[0]
    lr_off = GLA_R_OFF + GLA_DV
    w_in = jnp.concatenate([w[:, :lr_off], w[:, lr_off + GLA_GATE_RANK:], w[:, lr_off:lr_off + GLA_GATE_RANK]],
                           axis=1)
    mk1, mv1, mkt, mvh = mem_kv(1)
    zp = norm_matmul(hp, g_mix[1], w_in.astype(BF16))
    tok_p, st_p = gla_prompt(zp.reshape(B, T, -1), w_gate_up[0], b_gate[0], g_gla_onorm[0])
    xa_p = cross_prompt(zp[:, GLA_XQ_OFF2:GLA_XQ_OFF2 + X_W], mkt, mvh, B)
    hp = outproj(tok_p.reshape(B * T, GLA_DV), xa_p, hp, w_out[1].astype(BF16))

    zs = norm_matmul(hs, g_mix[1], w_in)
    tok_s, st_s = gla_sample(zs, w_gate_up[0], b_gate[0], g_gla_onorm[0],
                             state_gla[0].reshape(Bs, GLA_DK, GLA_HDV))
    hs = outproj(tok_s, cross_s(zs[:, GLA_XQ_OFF2:GLA_XQ_OFF2 + X_W], 1), hs, w_out[1])

    w_gu = w_exp_gu[0].astype(BF16)
    w_dn = w_exp_down[0].astype(BF16)
    y_p = moe_final(hp, g_ffn[1], w_router[0], w_gu, w_dn, g_final, tm=512)
    y_s = moe_final(hs, g_ffn[1], w_router[0], w_gu, w_dn, g_final, tm=128)

    st_p = jnp.stack([jnp.stack([st_p[b, h * GLA_HDV:(h + 1) * GLA_HDV, h * GLA_HDK:(h + 1) * GLA_HDK].T
                                 for h in range(GLA_HEADS)]) for b in range(B)])
    return (y_p.reshape(B, T, D), y_s.reshape(Bs, 1, D),
            k_p.reshape(1, B, T, SB_HEADS, HEAD_DIM), v_p.reshape(1, B, T, SB_HEADS, HEAD_DIM),
            k_s.reshape(1, Bs, 1, SB_HEADS, HEAD_DIM), v_s.reshape(1, Bs, 1, SB_HEADS, HEAD_DIM),
            st_p[None], st_s.reshape(1, Bs, GLA_HEADS, GLA_HDK, GLA_HDV),
            jnp.stack([mk0, mk1]), jnp.stack([mv0, mv1]))
```
